```python
import jax, jax.numpy as jnp
from jax import lax
import numpy as np

D_MODEL = 1024
BATCH = 32
SEQ = 2048
DEPTH = 4
DEC_BATCH = 16
DEC_SEQ = 4096
PAST_LEN = 128

HEAD_DIM = 64
N_HEADS_NA = 8
N_HEADS_DN = 8
D_NA = N_HEADS_NA * HEAD_DIM
D_DN = N_HEADS_DN * HEAD_DIM
D_MIX = D_NA + D_DN
GRID_W = 64
WIN_H_MAX = 8
WIN_W = 16
QB_W = 16
KB_W = QB_W + WIN_W
SHORT_CONV_W = 3
CHUNK = 64
N_MEM = 256
N_HEADS_X = 4
HEAD_DIM_X = D_MODEL // N_HEADS_X
D_FF = 2816
FFN_CONV_W = 3
RMS_EPS = 1e-6
IN_WIDTHS = (D_NA, D_NA, D_NA, 3 * D_DN, D_DN, N_HEADS_DN, N_HEADS_DN, N_HEADS_DN, N_HEADS_DN)
D_IN = 3 * D_NA + 4 * D_DN + 4 * N_HEADS_DN

kernel_name = "hybrid_natten_bigdn_encoder"


def _rmsnorm(x, g):
    xf = x.astype(jnp.float32)
    y = xf * lax.rsqrt(jnp.mean(xf * xf, axis=-1, keepdims=True) + RMS_EPS)
    return (y * g.astype(jnp.float32)).astype(x.dtype)


def _l2norm(x):
    return x * lax.rsqrt(jnp.sum(x * x, axis=-1, keepdims=True) + 1e-6)


def _dwconv_centred(x, w):
    k = w.shape[0]
    return lax.conv_general_dilated(x, w[:, None, :], window_strides=(1,), padding=[(k // 2, k // 2)],
                                    dimension_numbers=('NWC', 'WIO', 'NWC'), feature_group_count=x.shape[-1])


def _na_col_tables():
    n_cb = GRID_W // QB_W
    kc0 = np.clip(np.arange(n_cb) * QB_W - WIN_W // 2, 0, GRID_W - KB_W)
    col_idx = kc0[:, None] + np.arange(KB_W)[None, :]
    qcol = np.arange(GRID_W).reshape(n_cb, QB_W)
    cs = np.clip(qcol - WIN_W // 2, 0, GRID_W - WIN_W)[..., None]
    kcol = col_idx[:, None, :]
    col_mask = (kcol >= cs) & (kcol < cs + WIN_W)
    col_off = np.clip(kcol - qcol[..., None] + (WIN_W - 1), 0, 2 * WIN_W - 2)
    return col_idx, col_mask, col_off


def _neighbourhood_attention(q, k, v, rpb):
    b, t, _ = q.shape
    rows = t // GRID_W
    kh = min(WIN_H_MAX, rows)
    n_cb = GRID_W // QB_W
    col_idx, col_mask, col_off = _na_col_tables()
    grid = (b, rows, GRID_W, N_HEADS_NA, HEAD_DIM)
    qg = q.reshape(grid) * (HEAD_DIM ** -0.5)
    kg = k.reshape(grid)
    vg = v.reshape(grid)
    col_bias = rpb[:, :, col_off]
    mask = col_mask[:, :, None, :]

    def one_row(r):
        rs = jnp.clip(r - kh // 2, 0, rows - kh)
        kr = lax.dynamic_slice_in_dim(kg, rs, kh, axis=1)[:, :, col_idx]
        vr = lax.dynamic_slice_in_dim(vg, rs, kh, axis=1)[:, :, col_idx]
        qr = lax.dynamic_index_in_dim(qg, r, axis=1, keepdims=False).reshape(b, n_cb, QB_W, N_HEADS_NA, HEAD_DIM)
        s = jnp.einsum('bjqhd,bijchd->bhjqic', qr, kr).astype(jnp.float32)
        row_off = rs + jnp.arange(kh) - r + (WIN_H_MAX - 1)
        bias = jnp.take(col_bias, row_off, axis=1)
        s = s + jnp.transpose(bias, (0, 2, 3, 1, 4)).astype(jnp.float32)
        s = jnp.where(mask, s, -jnp.inf)
        p = jax.nn.softmax(s, axis=(-2, -1)).astype(vr.dtype)
        o = jnp.einsum('bhjqic,bijchd->bjqhd', p, vr)
        return o.reshape(b, GRID_W, D_NA)

    out = lax.map(one_row, jnp.arange(rows))
    return jnp.transpose(out, (1, 0, 2, 3)).reshape(b, t, D_NA)


def _gated_delta_chunked(q, k, v, g, beta):
    b, t, h, dk = q.shape
    dv = v.shape[-1]
    n = t // CHUNK

    def chunks(a):
        a = jnp.moveaxis(a, 2, 1)
        return a.reshape((b, h, n, CHUNK) + a.shape[3:])

    q, k, v, g, beta = (chunks(a) for a in (q, k, v, g, beta))
    gc = jnp.cumsum(g, axis=-1)
    tril = np.tril(np.ones((CHUNK, CHUNK), dtype=bool))
    strict = np.tril(np.ones((CHUNK, CHUNK), dtype=bool), -1)
    decay = jnp.exp(jnp.where(tril, gc[..., :, None] - gc[..., None, :], -jnp.inf))
    kb = k * beta[..., None]
    lmat = jnp.where(strict, jnp.einsum('bhncd,bhnsd->bhncs', kb, k) * decay, 0.0)
    eye = jnp.broadcast_to(jnp.eye(CHUNK, dtype=jnp.float32), lmat.shape)
    tinv = lax.linalg.triangular_solve(lmat, eye, left_side=True, lower=True, unit_diagonal=True)
    u = jnp.einsum('bhncs,bhnsd->bhncd', tinv, v * beta[..., None])
    w = jnp.einsum('bhncs,bhnsd->bhncd', tinv, kb * jnp.exp(gc)[..., None])
    qk = jnp.einsum('bhncd,bhnsd->bhncs', q, k) * decay
    qg = q * jnp.exp(gc)[..., None]
    kg = k * jnp.exp(gc[..., -1:] - gc)[..., None]
    glast = jnp.exp(gc[..., -1])

    def step(state, xs):
        qk_i, qg_i, w_i, u_i, kg_i, gl_i = xs
        v_new = u_i - jnp.einsum('bhcd,bhde->bhce', w_i, state)
        o_i = jnp.einsum('bhcd,bhde->bhce', qg_i, state) + jnp.einsum('bhcs,bhse->bhce', qk_i, v_new)
        state = state * gl_i[..., None, None] + jnp.einsum('bhcd,bhce->bhde', kg_i, v_new)
        return state, o_i

    xs = tuple(jnp.moveaxis(a, 2, 0) for a in (qk, qg, w, u, kg, glast))
    s0 = jnp.zeros((b, h, dk, dv), jnp.float32)
    _, o = lax.scan(step, s0, xs)
    return jnp.transpose(o, (1, 0, 3, 2, 4)).reshape(b, t, h, dv)


def _bidir_gated_deltanet(qkv, z, b_f, b_b, a_f, a_b, conv_w, a_log, dt_bias, norm_o):
    bsz, t, _ = qkv.shape
    out_dtype = z.dtype
    hs = (bsz, t, N_HEADS_DN, HEAD_DIM)
    act = jax.nn.silu(_dwconv_centred(qkv, conv_w)).astype(jnp.float32)
    q, k, v = jnp.split(act, 3, axis=-1)
    q = _l2norm(q.reshape(hs)) * (HEAD_DIM ** -0.5)
    k = _l2norm(k.reshape(hs))
    v = v.reshape(hs)

    def gates(a_raw, b_raw, d):
        g = -jnp.exp(a_log[d].astype(jnp.float32)) * jax.nn.softplus(a_raw.astype(jnp.float32) + dt_bias[d].astype(jnp.float32))
        return g, jax.nn.sigmoid(b_raw.astype(jnp.float32))

    g_f, beta_f = gates(a_f, b_f, 0)
    g_b, beta_b = gates(a_b, b_b, 1)
    flip = lambda a: jnp.flip(a, axis=1)
    o_f = _gated_delta_chunked(q, k, v, g_f, beta_f)
    o_b = flip(_gated_delta_chunked(flip(q), flip(k), flip(v), flip(g_b), flip(beta_b)))
    o = o_f + o_b
    o = o * lax.rsqrt(jnp.mean(o * o, axis=-1, keepdims=True) + RMS_EPS) * norm_o.astype(jnp.float32)
    o = o * jax.nn.silu(z.astype(jnp.float32).reshape(hs))
    return o.reshape(bsz, t, D_DN).astype(out_dtype)


def _memory_attention(xn, mem, norm_mem, w_q, w_kv, w_o):
    b, t, _ = xn.shape
    q = (xn @ w_q).reshape(b, t, N_HEADS_X, HEAD_DIM_X)
    k, v = jnp.split(_rmsnorm(mem, norm_mem) @ w_kv, 2, axis=-1)
    k = k.reshape(b, N_MEM, N_HEADS_X, HEAD_DIM_X)
    v = v.reshape(b, N_MEM, N_HEADS_X, HEAD_DIM_X)
    s = jnp.einsum('bthd,bmhd->bhtm', q, k).astype(jnp.float32) * (HEAD_DIM_X ** -0.5)
    p = jax.nn.softmax(s, axis=-1).astype(v.dtype)
    o = jnp.einsum('bhtm,bmhd->bthd', p, v).reshape(b, t, D_MODEL)
    return o @ w_o


def _conv_glu_ffn(xn, w_up, conv_w, conv_b, w_down):
    h = _dwconv_centred(xn @ w_up, conv_w) + conv_b
    val, gate = jnp.split(h, 2, axis=-1)
    return (jax.nn.silu(gate) * val) @ w_down


def _trunk(x, mem, norm_mix, w_in, rpb, conv_qkv, a_log, dt_bias, norm_o, w_out,
           norm_x, norm_mem, w_xq, w_xkv, w_xo, norm_ffn, w_up, conv_ffn, conv_ffn_b, w_down, norm_final):
    splits = list(np.cumsum(IN_WIDTHS)[:-1])
    for l in range(DEPTH):
        proj = _rmsnorm(x, norm_mix[l]) @ w_in[l]
        q_na, k_na, v_na, qkv_dn, z_dn, b_f, b_b, a_f, a_b = jnp.split(proj, splits, axis=-1)
        y_na = _neighbourhood_attention(q_na, k_na, v_na, rpb[l])
        y_dn = _bidir_gated_deltanet(qkv_dn, z_dn, b_f, b_b, a_f, a_b, conv_qkv[l], a_log[l], dt_bias[l], norm_o[l])
        x = x + jnp.concatenate([y_na, y_dn], axis=-1) @ w_out[l]
        x = x + _memory_attention(_rmsnorm(x, norm_x[l]), mem, norm_mem[l], w_xq[l], w_xkv[l], w_xo[l])
        x = x + _conv_glu_ffn(_rmsnorm(x, norm_ffn[l]), w_up[l], conv_ffn[l], conv_ffn_b[l], w_down[l])
    return _rmsnorm(x, norm_final)


def setup_inputs(seed: int = 0) -> dict:
    key = jax.random.key(seed)
    ks = jax.random.split(key, 24)
    f32 = jnp.float32

    def nrm(k, shape, scale):
        return jax.random.normal(k, shape, f32) * scale

    def gain(k, shape):
        return 1.0 + 0.02 * jax.random.normal(k, shape, f32)

    dt = jnp.exp(jax.random.uniform(ks[9], (DEPTH, 2, N_HEADS_DN), f32, np.log(1e-3), np.log(1e-1)))
    return {
        'x_prompt': nrm(ks[0], (BATCH, SEQ, D_MODEL), 1.0),
        'x_sample': nrm(ks[1], (DEC_BATCH, DEC_SEQ, D_MODEL), 1.0),
        'mem_prompt': nrm(ks[2], (BATCH, N_MEM, D_MODEL), 1.0),
        'mem_sample': nrm(ks[3], (DEC_BATCH, N_MEM, D_MODEL), 1.0),
        'norm_mix': gain(ks[4], (DEPTH, D_MODEL)),
        'w_in': nrm(ks[5], (DEPTH, D_MODEL, D_IN), D_MODEL ** -0.5),
        'rpb': nrm(ks[6], (DEPTH, N_HEADS_NA, 2 * WIN_H_MAX - 1, 2 * WIN_W - 1), 0.5),
        'conv_qkv': nrm(ks[7], (DEPTH, SHORT_CONV_W, 3 * D_DN), SHORT_CONV_W ** -0.5),
        'a_log': jnp.log(jax.random.uniform(ks[8], (DEPTH, 2, N_HEADS_DN), f32, 1.0, 16.0)),
        'dt_bias': dt + jnp.log(-jnp.expm1(-dt)),
        'norm_o': gain(ks[10], (DEPTH, HEAD_DIM)),
        'w_out': nrm(ks[11], (DEPTH, D_MIX, D_MODEL), D_MIX ** -0.5),
        'norm_x': gain(ks[12], (DEPTH, D_MODEL)),
        'norm_mem': gain(ks[13], (DEPTH, D_MODEL)),
        'w_xq': nrm(ks[14], (DEPTH, D_MODEL, D_MODEL), D_MODEL ** -0.5),
        'w_xkv': nrm(ks[15], (DEPTH, D_MODEL, 2 * D_MODEL), D_MODEL ** -0.5),
        'w_xo': nrm(ks[16], (DEPTH, D_MODEL, D_MODEL), D_MODEL ** -0.5),
        'norm_ffn': gain(ks[17], (DEPTH, D_MODEL)),
        'w_up': nrm(ks[18], (DEPTH, D_MODEL, 2 * D_FF), D_MODEL ** -0.5),
        'conv_ffn': nrm(ks[19], (DEPTH, FFN_CONV_W, 2 * D_FF), FFN_CONV_W ** -0.5),
        'conv_ffn_b': nrm(ks[20], (DEPTH, 2 * D_FF), 0.02),
        'w_down': nrm(ks[21], (DEPTH, D_FF, D_MODEL), D_FF ** -0.5),
        'norm_final': gain(ks[22], (D_MODEL,)),
    }


def reference(x_prompt, x_sample, mem_prompt, mem_sample, norm_mix, w_in, rpb, conv_qkv, a_log, dt_bias, norm_o,
              w_out, norm_x, norm_mem, w_xq, w_xkv, w_xo, norm_ffn, w_up, conv_ffn, conv_ffn_b, w_down, norm_final):
    weights = (norm_mix, w_in, rpb, conv_qkv, a_log, dt_bias, norm_o, w_out, norm_x, norm_mem, w_xq, w_xkv, w_xo,
               norm_ffn, w_up, conv_ffn, conv_ffn_b, w_down, norm_final)
    y_prompt = _trunk(x_prompt, mem_prompt, *weights)
    y_sample = _trunk(x_sample, mem_sample, *weights)
    return (y_prompt, y_sample)
```

```python
import functools

import numpy as np
import jax
import jax.numpy as jnp
from jax import lax
from jax.experimental import pallas as pl
from jax.experimental.pallas import tpu as pltpu

F32 = jnp.float32
BF16 = jnp.bfloat16

D_MODEL = 1024
DEPTH = 4
HEAD_DIM = 64
N_HEADS_NA = 8
N_HEADS_DN = 8
D_NA = N_HEADS_NA * HEAD_DIM
D_DN = N_HEADS_DN * HEAD_DIM
GRID_W = 64
WIN_H = 8
WIN_W = 16
CHUNK = 64
N_MEM = 256
N_HEADS_X = 4
HEAD_DIM_X = D_MODEL // N_HEADS_X
D_FF = 2816
RMS_EPS = 1e-6
L2_EPS = 1e-6

LANES = 128
N_PAIRS = N_HEADS_DN // 2
D_MAIN = 3 * D_NA + 4 * D_DN
N_GATE = 4 * N_HEADS_DN
FF_CHUNK = 256
N_FF_CHUNKS = D_FF // FF_CHUNK
HALO = 8
VMEM_LIMIT = 56 * 1024 * 1024


def _token_block(t):
    return min(512, t)


def _rms(x, g):
    return x * lax.rsqrt(jnp.mean(x * x, axis=-1, keepdims=True) + RMS_EPS) * g


def _sigmoid(x):
    return 1.0 / (1.0 + jnp.exp(-x))


def _split2(x):
    hi = x.astype(BF16)
    lo = (x - hi.astype(F32)).astype(BF16)
    return hi, lo


def _dot(a, b):
    return jnp.dot(a, b, preferred_element_type=F32)


def _dot_nt(a, b):
    return lax.dot_general(a, b, (((1,), (1,)), ((), ())), preferred_element_type=F32)


def _dot_tn(a, b):
    return lax.dot_general(a, b, (((0,), (0,)), ((), ())), preferred_element_type=F32)


def _const_spec(shape):
    nd = len(shape)
    return pl.BlockSpec(shape, lambda *_: (0,) * nd)


def _params(sem):
    return pltpu.CompilerParams(dimension_semantics=sem, vmem_limit_bytes=VMEM_LIMIT)


def _in_proj_kernel(x_ref, g_ref, wm_ref, wg_ref, na_ref, dn_ref, z_ref, gate_ref):
    xn = _rms(x_ref[...], g_ref[...]).astype(BF16)
    for j in range(3):
        r = _dot(xn, wm_ref[:, j * D_NA:(j + 1) * D_NA])
        if j == 0:
            r = r * (HEAD_DIM ** -0.5)
        na_ref[:, j * D_NA:(j + 1) * D_NA] = r.astype(na_ref.dtype)
    off = 3 * D_NA
    for j in range(3):
        r = _dot(xn, wm_ref[:, off + j * D_DN:off + (j + 1) * D_DN])
        dn_ref[:, j * D_DN:(j + 1) * D_DN] = r.astype(dn_ref.dtype)
    off = 3 * D_NA + 3 * D_DN
    z_ref[...] = _dot(xn, wm_ref[:, off:off + D_DN]).astype(z_ref.dtype)
    gate_ref[...] = _dot(xn, wg_ref[...])


def _in_proj(x2d, g, w_main, w_gate):
    n = x2d.shape[0]
    tm = _token_block(n)
    return pl.pallas_call(
        _in_proj_kernel,
        grid=(n // tm,),
        in_specs=[
            pl.BlockSpec((tm, D_MODEL), lambda i: (i, 0)),
            _const_spec((1, D_MODEL)),
            _const_spec((D_MODEL, D_MAIN)),
            _const_spec((D_MODEL, LANES)),
        ],
        out_specs=[
            pl.BlockSpec((tm, 3 * D_NA), lambda i: (i, 0)),
            pl.BlockSpec((tm, 3 * D_DN), lambda i: (i, 0)),
            pl.BlockSpec((tm, D_DN), lambda i: (i, 0)),
            pl.BlockSpec((tm, LANES), lambda i: (i, 0)),
        ],
        out_shape=[
            jax.ShapeDtypeStruct((n, 3 * D_NA), BF16),
            jax.ShapeDtypeStruct((n, 3 * D_DN), BF16),
            jax.ShapeDtypeStruct((n, D_DN), BF16),
            jax.ShapeDtypeStruct((n, LANES), F32),
        ],
        compiler_params=_params(("parallel",)),
        name="in_proj",
    )(x2d, g, w_main, w_gate)


def _na_bias_tables(rpb_l):
    c = np.arange(GRID_W)
    cs = np.clip(c - WIN_W // 2, 0, GRID_W - WIN_W)
    valid = (c[None, :] >= cs[:, None]) & (c[None, :] < cs[:, None] + WIN_W)
    coff = np.clip(c[None, :] - c[:, None] + (WIN_W - 1), 0, 2 * WIN_W - 2)
    roff = np.arange(WIN_H)[None, :] + (WIN_H - 1) - np.arange(WIN_H)[:, None]
    b = rpb_l[:, roff][:, :, :, coff]
    b = jnp.where(valid[None, None, None], b, -jnp.inf)
    b = jnp.transpose(b, (1, 0, 3, 2, 4))
    return b.reshape(WIN_H, N_PAIRS, 2 * GRID_W, WIN_H * GRID_W).astype(F32)


def _na_kernel(q_ref, k_ref, v_ref, bias_ref, o_ref, *, rows):
    r = pl.program_id(1)
    rs = jnp.clip(r - WIN_H // 2, 0, rows - WIN_H)
    start = pl.multiple_of(rs * GRID_W, GRID_W)
    nk = WIN_H * GRID_W
    lane = lax.broadcasted_iota(jnp.int32, (GRID_W, LANES), 1)
    low = lane < HEAD_DIM
    for p in range(N_PAIRS):
        cols = slice(p * LANES, (p + 1) * LANES)
        q2 = q_ref[:, cols].astype(F32)
        k2 = k_ref[pl.ds(start, nk), cols]
        v2 = v_ref[pl.ds(start, nk), cols]
        qs = jnp.concatenate([jnp.where(low, q2, 0.0), jnp.where(low, 0.0, q2)], axis=0).astype(BF16)
        s = _dot_nt(qs, k2) + bias_ref[p]
        m = jnp.max(s, axis=-1, keepdims=True)
        e = jnp.exp(s - m)
        l = jnp.sum(e, axis=-1, keepdims=True)
        o = _dot(e.astype(BF16), v2) * (1.0 / l)
        o_ref[:, cols] = jnp.where(low, o[:GRID_W], o[GRID_W:]).astype(o_ref.dtype)


def _na(qkv, bias, b, t):
    rows = t // GRID_W
    qkv3 = qkv.reshape(b, t, 3 * D_NA)

    def bias_map(i, r):
        return (r - jnp.clip(r - WIN_H // 2, 0, rows - WIN_H), 0, 0, 0)

    return pl.pallas_call(
        functools.partial(_na_kernel, rows=rows),
        grid=(b, rows),
        in_specs=[
            pl.BlockSpec((None, GRID_W, D_NA), lambda i, r: (i, r, 0)),
            pl.BlockSpec((None, t, D_NA), lambda i, r: (i, 0, 1)),
            pl.BlockSpec((None, t, D_NA), lambda i, r: (i, 0, 2)),
            pl.BlockSpec((None, N_PAIRS, 2 * GRID_W, WIN_H * GRID_W), bias_map),
        ],
        out_specs=pl.BlockSpec((None, GRID_W, D_NA), lambda i, r: (i, r, 0)),
        out_shape=jax.ShapeDtypeStruct((b, t, D_NA), BF16),
        compiler_params=_params(("parallel", "arbitrary")),
        name="na",
    )(qkv3, qkv3, qkv3, bias)


def _pair_masks(rows):
    lane = lax.broadcasted_iota(jnp.int32, (rows, LANES), 1)
    return lane < HEAD_DIM


def _block_diag(x, low):
    return jnp.concatenate([jnp.where(low, x, 0.0), jnp.where(low, 0.0, x)], axis=0)


def _seg_sum(x, ones_bd):
    hi, lo = _split2(x)
    return _dot(hi, ones_bd) + _dot(lo, ones_bd)


def _dn_kernel(q_ref, k_ref, v_ref, z_ref, gate_ref, cwq_ref, cwk_ref, cwv_ref, prm_ref, normo_ref, o_ref,
               qs, ks, vs, gcmp, u_s, w_s, qk_s, qg_s, kg_s, gl_s, of_s, ob_s, *, t):
    n = t // CHUNK
    pair = pl.program_id(1)
    row_t = lax.broadcasted_iota(jnp.int32, (t, LANES), 0)
    r128 = lax.broadcasted_iota(jnp.int32, (LANES, LANES), 0)
    c128 = lax.broadcasted_iota(jnp.int32, (LANES, LANES), 1)
    ones_bd = ((r128 // HEAD_DIM) == (c128 // HEAD_DIM)).astype(BF16)

    def conv_silu(x_ref, w_ref):
        x = x_ref[...].astype(F32)
        w = w_ref[...]
        xp = jnp.where(row_t == 0, 0.0, pltpu.roll(x, 1, 0))
        xn = jnp.where(row_t == t - 1, 0.0, pltpu.roll(x, t - 1, 0))
        y = xp * w[0:1] + x * w[1:2] + xn * w[2:3]
        return y * _sigmoid(y)

    def l2norm(x):
        return x * lax.rsqrt(_seg_sum(x * x, ones_bd) + L2_EPS)

    qs[...] = l2norm(conv_silu(q_ref, cwq_ref)) * (HEAD_DIM ** -0.5)
    ks[...] = l2norm(conv_silu(k_ref, cwk_ref))
    vs[...] = conv_silu(v_ref, cwv_ref)

    gate = gate_ref[...]
    col_t = lax.broadcasted_iota(jnp.int32, (t, LANES), 1)
    xx = gate + prm_ref[1:2]
    softplus = jnp.maximum(xx, 0.0) + jnp.log(1.0 + jnp.exp(-jnp.abs(xx)))
    gcmp[...] = jnp.where(col_t < 2 * N_HEADS_DN, _sigmoid(gate), -jnp.exp(prm_ref[0:1]) * softplus)

    er = lax.broadcasted_iota(jnp.int32, (LANES, 4 * LANES), 0)
    ec = lax.broadcasted_iota(jnp.int32, (LANES, 4 * LANES), 1)
    src = (ec // LANES) * N_HEADS_DN + 2 * pair + (ec % LANES) // HEAD_DIM
    spread = (er == src).astype(BF16)
    spread2 = jnp.concatenate([spread, spread], axis=0)

    low = _pair_masks(CHUNK)
    ri = lax.broadcasted_iota(jnp.int32, (CHUNK, LANES), 0)
    ci = lax.broadcasted_iota(jnp.int32, (CHUNK, LANES), 1) % HEAD_DIM
    eye2 = (ri == ci).astype(F32)
    ti = lax.broadcasted_iota(jnp.int32, (CHUNK, CHUNK), 0)
    tj = lax.broadcasted_iota(jnp.int32, (CHUNK, CHUNK), 1)
    neg_ones = jnp.full((CHUNK, CHUNK), -1.0, F32)

    def mm(a, b):
        return _dot(a.astype(BF16), b.astype(BF16))

    def neumann(nmat):
        p = mm(nmat, _block_diag(nmat, low))
        tt = eye2 + nmat
        for _ in range(4):
            out = mm(p, jnp.concatenate([_block_diag(p, low), _block_diag(tt, low)], axis=1))
            p = out[:, :LANES]
            tt = tt + out[:, LANES:]
        return tt + mm(p, _block_diag(tt, low))

    def phase_a(c, carry):
        r0 = pl.multiple_of(c * CHUNK, CHUNK)
        rows = pl.ds(r0, CHUNK)
        kc = ks[rows, :]
        qc = qs[rows, :]
        vc = vs[rows, :]
        ghi, glo = _split2(gcmp[rows, :])
        sp = _dot(jnp.concatenate([ghi, glo], axis=1), spread2)
        kq = jnp.concatenate([kc, qc], axis=0).astype(BF16)
        gq = _dot_nt(kq, _block_diag(kc, low).astype(BF16))
        gram = gq[:CHUNK]
        qk = gq[CHUNK:]
        for d in range(2):
            beta = sp[:, d * LANES:(d + 1) * LANES]
            g = sp[:, (2 + d) * LANES:(3 + d) * LANES]
            if d == 0:
                tri = (tj <= ti).astype(F32)
                keep = ci <= ri
                upto = ci >= ri
                strict = ci < ri
                last = CHUNK - 1
            else:
                tri = (tj >= ti).astype(F32)
                keep = ci >= ri
                upto = ci <= ri
                strict = ci > ri
                last = 0
            lhs = jnp.concatenate([tri, neg_ones, tri, neg_ones], axis=1).astype(BF16)
            ghi2, glo2 = _split2(g)
            zero = jnp.zeros_like(ghi2)
            rhs = jnp.concatenate([
                jnp.concatenate([ghi2, jnp.where(upto, ghi2, zero), glo2, jnp.where(upto, glo2, zero)], axis=0),
                jnp.concatenate([ghi2, zero, glo2, zero], axis=0)], axis=1)
            dg = _dot(lhs, rhs)
            delta = dg[:, :LANES]
            gcol = dg[:, LANES:]
            decay = jnp.where(keep, jnp.exp(jnp.minimum(delta, 0.0)), 0.0)
            gamma = jnp.exp(gcol)
            glast = gcol[last:last + 1, :]
            kdec = jnp.exp(glast - gcol)
            lmat = jnp.where(strict, beta * gram * decay, 0.0)
            tinv = neumann(-lmat)
            uw = mm(tinv, jnp.concatenate([_block_diag(vc * beta, low), _block_diag(kc * (beta * gamma), low)], axis=1))
            u_s[d, rows, :] = uw[:, :LANES]
            w_s[d, rows, :] = uw[:, LANES:].astype(BF16)
            qk_s[d, rows, :] = (qk * decay).astype(BF16)
            qg_s[d, rows, :] = (qc * gamma).astype(BF16)
            kg_s[d, rows, :] = (kc * kdec).astype(BF16)
            gl_s[d, c] = jnp.broadcast_to(jnp.exp(glast), (8, LANES))
        return carry

    lax.fori_loop(0, n, phase_a, 0)

    diag_blocks = (r128 // HEAD_DIM) == (c128 // HEAD_DIM)

    def step(d, c, state, out_s):
        r0 = pl.multiple_of(c * CHUNK, CHUNK)
        rows = pl.ds(r0, CHUNK)
        lhs = jnp.concatenate([w_s[d, rows, :], qg_s[d, rows, :]], axis=0)
        ws = _dot(lhs, state.astype(BF16))
        vn = u_s[d, rows, :] - ws[:CHUNK]
        vnb = vn.astype(BF16)
        out_s[rows, :] = ws[CHUNK:] + _dot(qk_s[d, rows, :], _block_diag(vn, low).astype(BF16))
        upd = _dot_tn(kg_s[d, rows, :], vnb)
        return state * gl_s[d, c][0:1, :] + jnp.where(diag_blocks, upd, 0.0)

    def phase_b(i, carry):
        sf, sb = carry
        sf = step(0, i, sf, of_s)
        sb = step(1, n - 1 - i, sb, ob_s)
        return sf, sb

    zero_state = jnp.zeros((LANES, LANES), F32)
    lax.fori_loop(0, n, phase_b, (zero_state, zero_state))

    o = of_s[...] + ob_s[...]
    ms = _seg_sum(o * o, ones_bd) * (1.0 / HEAD_DIM)
    o = o * lax.rsqrt(ms + RMS_EPS) * normo_ref[...]
    z = z_ref[...].astype(F32)
    o_ref[...] = (o * (z * _sigmoid(z))).astype(o_ref.dtype)


def _dn(qkv, z, gates, conv_w, prm, normo2, b, t):
    qkv3 = qkv.reshape(b, t, 3 * D_DN)
    z3 = z.reshape(b, t, D_DN)
    g3 = gates.reshape(b, t, LANES)
    n = t // CHUNK

    def col_spec(base):
        return pl.BlockSpec((None, t, LANES), lambda i, p: (i, 0, base + p))

    def cw_spec(base):
        return pl.BlockSpec((3, LANES), lambda i, p: (0, base + p))

    return pl.pallas_call(
        functools.partial(_dn_kernel, t=t),
        grid=(b, N_PAIRS),
        in_specs=[
            col_spec(0), col_spec(N_PAIRS), col_spec(2 * N_PAIRS),
            col_spec(0),
            pl.BlockSpec((None, t, LANES), lambda i, p: (i, 0, 0)),
            cw_spec(0), cw_spec(N_PAIRS), cw_spec(2 * N_PAIRS),
            _const_spec((2, LANES)),
            _const_spec((1, LANES)),
        ],
        out_specs=pl.BlockSpec((None, t, LANES), lambda i, p: (i, 0, p)),
        out_shape=jax.ShapeDtypeStruct((b, t, D_DN), BF16),
        scratch_shapes=[
            pltpu.VMEM((t, LANES), F32),
            pltpu.VMEM((t, LANES), F32),
            pltpu.VMEM((t, LANES), F32),
            pltpu.VMEM((t, LANES), F32),
            pltpu.VMEM((2, t, LANES), F32),
            pltpu.VMEM((2, t, LANES), BF16),
            pltpu.VMEM((2, t, LANES), BF16),
            pltpu.VMEM((2, t, LANES), BF16),
            pltpu.VMEM((2, t, LANES), BF16),
            pltpu.VMEM((2, n, 8, LANES), F32),
            pltpu.VMEM((t, LANES), F32),
            pltpu.VMEM((t, LANES), F32),
        ],
        compiler_params=_params(("parallel", "arbitrary")),
        name="dn",
    )(qkv3, qkv3, qkv3, z3, g3, conv_w, conv_w, conv_w, prm, normo2)


def _mem_kv_kernel(m_ref, g_ref, w_ref, k_ref, v_ref):
    mn = _rms(m_ref[...], g_ref[...]).astype(BF16)
    k_ref[...] = _dot(mn, w_ref[:, :D_MODEL]).astype(k_ref.dtype)
    v_ref[...] = _dot(mn, w_ref[:, D_MODEL:]).astype(v_ref.dtype)


def _mem_kv(mem, g, w_kv):
    b = mem.shape[0]
    blk = pl.BlockSpec((None, N_MEM, D_MODEL), lambda i: (i, 0, 0))
    return pl.pallas_call(
        _mem_kv_kernel,
        grid=(b,),
        in_specs=[blk, _const_spec((1, D_MODEL)), _const_spec((D_MODEL, 2 * D_MODEL))],
        out_specs=[blk, blk],
        out_shape=[jax.ShapeDtypeStruct((b, N_MEM, D_MODEL), BF16)] * 2,
        compiler_params=_params(("parallel",)),
        name="mem_kv",
    )(mem, g, w_kv)


def _mix_xattn_kernel(x_ref, yna_ref, ydn_ref, wout_ref, g_ref, wq_ref, k_ref, v_ref, wo_ref, o_ref):
    x = x_ref[...] + _dot(yna_ref[...], wout_ref[:D_NA, :]) + _dot(ydn_ref[...], wout_ref[D_NA:, :])
    xn = _rms(x, g_ref[...]).astype(BF16)
    q = (_dot(xn, wq_ref[...]) * (HEAD_DIM_X ** -0.5)).astype(BF16)
    heads = []
    for h in range(N_HEADS_X):
        cols = slice(h * HEAD_DIM_X, (h + 1) * HEAD_DIM_X)
        s = _dot_nt(q[:, cols], k_ref[:, cols])
        m = jnp.max(s, axis=-1, keepdims=True)
        e = jnp.exp(s - m)
        l = jnp.sum(e, axis=-1, keepdims=True)
        heads.append((_dot(e.astype(BF16), v_ref[:, cols]) * (1.0 / l)).astype(BF16))
    o_ref[...] = x + _dot(jnp.concatenate(heads, axis=-1), wo_ref[...])


def _mix_xattn(x, y_na, y_dn, w_out, g, w_q, kmem, vmem, w_o):
    b, t, _ = x.shape
    tm = _token_block(t)
    tok = lambda d: pl.BlockSpec((None, tm, d), lambda i, j: (i, j, 0))
    memspec = pl.BlockSpec((None, N_MEM, D_MODEL), lambda i, j: (i, 0, 0))
    return pl.pallas_call(
        _mix_xattn_kernel,
        grid=(b, t // tm),
        in_specs=[
            tok(D_MODEL), tok(D_NA), tok(D_DN),
            _const_spec((D_NA + D_DN, D_MODEL)),
            _const_spec((1, D_MODEL)),
            _const_spec((D_MODEL, D_MODEL)),
            memspec, memspec,
            _const_spec((D_MODEL, D_MODEL)),
        ],
        out_specs=tok(D_MODEL),
        out_shape=jax.ShapeDtypeStruct((b, t, D_MODEL), F32),
        compiler_params=_params(("parallel", "parallel")),
        name="mix_xattn",
    )(x, y_na, y_dn, w_out, g, w_q, kmem, vmem, w_o)


def _ffn_kernel(x_ref, xp_ref, xnx_ref, g_ref, wv_ref, wg_ref, cv_ref, cg_ref, bv_ref, bg_ref, wd_ref, gf_ref, o_ref,
                *, tm, final):
    j = pl.program_id(1)
    nj = pl.num_programs(1)
    g = g_ref[...]
    x = x_ref[...]
    has_prev = jnp.where(j > 0, 1.0, 0.0)
    has_next = jnp.where(j < nj - 1, 1.0, 0.0)
    xe = jnp.concatenate([_rms(xp_ref[...], g) * has_prev, _rms(x, g), _rms(xnx_ref[...], g) * has_next],
                         axis=0).astype(BF16)
    ext = tm + 2 * HALO

    def conv(h, w, bias):
        hp = pltpu.roll(h, 1, 0)[HALO:HALO + tm]
        hn = pltpu.roll(h, ext - 1, 0)[HALO:HALO + tm]
        return hp * w[0:1] + h[HALO:HALO + tm] * w[1:2] + hn * w[2:3] + bias

    acc = x
    for c in range(N_FF_CHUNKS):
        val = conv(_dot(xe, wv_ref[c]), cv_ref[c], bv_ref[c])
        gate = conv(_dot(xe, wg_ref[c]), cg_ref[c], bg_ref[c])
        act = (gate * _sigmoid(gate) * val).astype(BF16)
        acc = acc + _dot(act, wd_ref[c])
    if final:
        acc = _rms(acc, gf_ref[...])
    o_ref[...] = acc


def _ffn(x, g, wv, wg, cv, cg, bv, bg, wd, g_final, final):
    b, t, _ = x.shape
    tm = _token_block(t)
    nh = tm // HALO
    last = t // HALO - 1
    return pl.pallas_call(
        functools.partial(_ffn_kernel, tm=tm, final=final),
        grid=(b, t // tm),
        in_specs=[
            pl.BlockSpec((None, tm, D_MODEL), lambda i, j: (i, j, 0)),
            pl.BlockSpec((None, HALO, D_MODEL), lambda i, j: (i, jnp.maximum(j * nh - 1, 0), 0)),
            pl.BlockSpec((None, HALO, D_MODEL), lambda i, j: (i, jnp.minimum((j + 1) * nh, last), 0)),
            _const_spec((1, D_MODEL)),
            _const_spec((N_FF_CHUNKS, D_MODEL, FF_CHUNK)),
            _const_spec((N_FF_CHUNKS, D_MODEL, FF_CHUNK)),
            _const_spec((N_FF_CHUNKS, 3, FF_CHUNK)),
            _const_spec((N_FF_CHUNKS, 3, FF_CHUNK)),
            _const_spec((N_FF_CHUNKS, 1, FF_CHUNK)),
            _const_spec((N_FF_CHUNKS, 1, FF_CHUNK)),
            _const_spec((N_FF_CHUNKS, FF_CHUNK, D_MODEL)),
            _const_spec((1, D_MODEL)),
        ],
        out_specs=pl.BlockSpec((None, tm, D_MODEL), lambda i, j: (i, j, 0)),
        out_shape=jax.ShapeDtypeStruct((b, t, D_MODEL), F32),
        compiler_params=_params(("parallel", "parallel")),
        name="ffn",
    )(x, x, x, g, wv, wg, cv, cg, bv, bg, wd, g_final)


def _chunk_cols(w):
    rows = w.shape[0]
    parts = w.reshape(rows, 2, N_FF_CHUNKS, FF_CHUNK)
    return jnp.transpose(parts[:, 0], (1, 0, 2)), jnp.transpose(parts[:, 1], (1, 0, 2))


def _prep_layer(l, norm_mix, w_in, rpb, conv_qkv, a_log, dt_bias, norm_o, w_out, norm_x, norm_mem, w_xq, w_xkv, w_xo,
                norm_ffn, w_up, conv_ffn, conv_ffn_b, w_down):
    row = lambda v: v.reshape(1, -1).astype(F32)
    w_gate = jnp.pad(w_in[l][:, D_MAIN:], ((0, 0), (0, LANES - N_GATE))).astype(BF16)
    pad = jnp.zeros((2 * N_HEADS_DN,), F32)
    tail = jnp.zeros((LANES - N_GATE,), F32)
    prm = jnp.stack([jnp.concatenate([pad, a_log[l].reshape(-1), tail]),
                     jnp.concatenate([pad, dt_bias[l].reshape(-1), tail])])
    wv, wg = _chunk_cols(w_up[l].astype(BF16))
    cv, cg = _chunk_cols(conv_ffn[l])
    bv, bg = _chunk_cols(conv_ffn_b[l].reshape(1, -1))
    return dict(
        norm_mix=row(norm_mix[l]), w_main=w_in[l][:, :D_MAIN].astype(BF16), w_gate=w_gate,
        na_bias=_na_bias_tables(rpb[l]), conv_qkv=conv_qkv[l], prm=prm,
        norm_o=row(jnp.concatenate([norm_o[l], norm_o[l]])), w_out=w_out[l].astype(BF16),
        norm_x=row(norm_x[l]), norm_mem=row(norm_mem[l]), w_xq=w_xq[l].astype(BF16), w_xkv=w_xkv[l].astype(BF16),
        w_xo=w_xo[l].astype(BF16), norm_ffn=row(norm_ffn[l]), wv=wv, wg=wg, cv=cv, cg=cg, bv=bv, bg=bg,
        wd=w_down[l].astype(BF16).reshape(N_FF_CHUNKS, FF_CHUNK, D_MODEL))


def _layer(x, mem, p, g_final, final):
    b, t, _ = x.shape
    qkv_na, qkv_dn, z, gates = _in_proj(x.reshape(b * t, D_MODEL), p["norm_mix"], p["w_main"], p["w_gate"])
    y_na = _na(qkv_na, p["na_bias"], b, t)
    y_dn = _dn(qkv_dn, z, gates, p["conv_qkv"], p["prm"], p["norm_o"], b, t)
    kmem, vmem = _mem_kv(mem, p["norm_mem"], p["w_xkv"])
    x = _mix_xattn(x, y_na, y_dn, p["w_out"], p["norm_x"], p["w_xq"], kmem, vmem, p["w_xo"])
    return _ffn(x, p["norm_ffn"], p["wv"], p["wg"], p["cv"], p["cg"], p["bv"], p["bg"], p["wd"], g_final, final)


def kernel(x_prompt, x_sample, mem_prompt, mem_sample, norm_mix, w_in, rpb, conv_qkv, a_log, dt_bias, norm_o, w_out,
           norm_x, norm_mem, w_xq, w_xkv, w_xo, norm_ffn, w_up, conv_ffn, conv_ffn_b, w_down, norm_final):
    layers = [_prep_layer(l, norm_mix, w_in, rpb, conv_qkv, a_log, dt_bias, norm_o, w_out, norm_x, norm_mem, w_xq,
                          w_xkv, w_xo, norm_ffn, w_up, conv_ffn, conv_ffn_b, w_down) for l in range(DEPTH)]
    g_final = norm_final.reshape(1, -1).astype(F32)
    outs = []
    for x, mem in ((x_prompt, mem_prompt), (x_sample, mem_sample)):
        for l in range(DEPTH):
            x = _layer(x, mem, layers[l], g_final, l == DEPTH - 1)
        outs.append(x)
    return tuple(outs)
```

```python
import functools

import numpy as np
import jax
import jax.numpy as jnp
from jax import lax
from jax.experimental import pallas as pl
from jax.experimental.pallas import tpu as pltpu

F32 = jnp.float32
BF16 = jnp.bfloat16

D_MODEL = 1024
DEPTH = 4
HEAD_DIM = 64
N_HEADS_NA = 8
N_HEADS_DN = 8
D_NA = N_HEADS_NA * HEAD_DIM
D_DN = N_HEADS_DN * HEAD_DIM
GRID_W = 64
WIN_H = 8
WIN_W = 16
CHUNK = 64
N_MEM = 256
N_HEADS_X = 4
HEAD_DIM_X = D_MODEL // N_HEADS_X
D_FF = 2816
RMS_EPS = 1e-6
L2_EPS = 1e-6

LANES = 128
MXU_DIM = 256
N_PAIRS = N_HEADS_DN // 2
D_MAIN = 3 * D_NA + 4 * D_DN
N_GATE = 4 * N_HEADS_DN
FF_CHUNK = 256
N_FF_CHUNKS = D_FF // FF_CHUNK
HALO = 8
NA_ROWS = 4
NA_LOCKSTEP = 8
DN_GROUP = 8
VMEM_LIMIT = 56 * 1024 * 1024


def _token_block(t):
    return min(512, t)


def _rms(x, g):
    return x * lax.rsqrt(jnp.mean(x * x, axis=-1, keepdims=True) + RMS_EPS) * g


def _sigmoid(x):
    return 1.0 / (1.0 + jnp.exp(-x))


def _split2(x):
    hi = x.astype(BF16)
    lo = (x - hi.astype(F32)).astype(BF16)
    return hi, lo


def _dot(a, b):
    return jnp.dot(a, b, preferred_element_type=F32)


def _dot_nt(a, b):
    return lax.dot_general(a, b, (((1,), (1,)), ((), ())), preferred_element_type=F32)


def _dot_tn(a, b):
    return lax.dot_general(a, b, (((0,), (0,)), ((), ())), preferred_element_type=F32)


def _head_block_ones(n):
    r = lax.broadcasted_iota(jnp.int32, (n, n), 0)
    c = lax.broadcasted_iota(jnp.int32, (n, n), 1)
    return ((r // HEAD_DIM) == (c // HEAD_DIM)).astype(BF16)


def _const_spec(shape):
    nd = len(shape)
    return pl.BlockSpec(shape, lambda *_: (0,) * nd)


def _params(sem):
    return pltpu.CompilerParams(dimension_semantics=sem, vmem_limit_bytes=VMEM_LIMIT)


def _halo_specs(tm, t, d):
    nh = tm // HALO
    last = t // HALO - 1
    return [
        pl.BlockSpec((None, tm, d), lambda i, j: (i, j, 0)),
        pl.BlockSpec((None, HALO, d), lambda i, j: (i, jnp.maximum(j * nh - 1, 0), 0)),
        pl.BlockSpec((None, HALO, d), lambda i, j: (i, jnp.minimum((j + 1) * nh, last), 0)),
    ]


def _normed_with_halo(x_ref, xp_ref, xnx_ref, g):
    j = pl.program_id(1)
    has_prev = jnp.where(j > 0, 1.0, 0.0)
    has_next = jnp.where(j < pl.num_programs(1) - 1, 1.0, 0.0)
    return jnp.concatenate([_rms(xp_ref[...], g) * has_prev, _rms(x_ref[...], g), _rms(xnx_ref[...], g) * has_next],
                           axis=0).astype(BF16)


def _token_conv3(h, w, tm):
    ext = tm + 2 * HALO
    hp = pltpu.roll(h, 1, 0)[HALO:HALO + tm]
    hn = pltpu.roll(h, ext - 1, 0)[HALO:HALO + tm]
    return hp * w[0:1] + h[HALO:HALO + tm] * w[1:2] + hn * w[2:3]


def _in_proj_kernel(x_ref, xp_ref, xnx_ref, g_ref, wm_ref, wg_ref, cw_ref, prm_ref, na_ref, dn_ref, z_ref, gate_ref,
                    *, tm):
    xe = _normed_with_halo(x_ref, xp_ref, xnx_ref, g_ref[...])
    xm = xe[HALO:HALO + tm]
    for j in range(3):
        r = _dot(xm, wm_ref[:, j * D_NA:(j + 1) * D_NA])
        if j == 0:
            r = r * (HEAD_DIM ** -0.5)
        na_ref[:, j * D_NA:(j + 1) * D_NA] = r.astype(na_ref.dtype)

    ones_bd = _head_block_ones(MXU_DIM)
    off = 3 * D_NA
    for c in range(3 * D_DN // MXU_DIM):
        cols = slice(c * MXU_DIM, (c + 1) * MXU_DIM)
        y = _token_conv3(_dot(xe, wm_ref[:, off + c * MXU_DIM:off + (c + 1) * MXU_DIM]), cw_ref[:, cols], tm)
        y = y * _sigmoid(y)
        if c < 2 * D_DN // MXU_DIM:
            y = y * lax.rsqrt(_dot((y * y).astype(BF16), ones_bd) + L2_EPS)
        if c < D_DN // MXU_DIM:
            y = y * (HEAD_DIM ** -0.5)
        dn_ref[:, cols] = y.astype(dn_ref.dtype)

    off = 3 * D_NA + 3 * D_DN
    z_ref[...] = _dot(xm, wm_ref[:, off:off + D_DN]).astype(z_ref.dtype)

    raw = _dot(xm, wg_ref[...])
    col = lax.broadcasted_iota(jnp.int32, raw.shape, 1)
    xx = raw + prm_ref[1:2]
    softplus = jnp.maximum(xx, 0.0) + jnp.log(1.0 + jnp.exp(-jnp.abs(xx)))
    gate_ref[...] = jnp.where(col < 2 * N_HEADS_DN, _sigmoid(raw), -jnp.exp(prm_ref[0:1]) * softplus)


def _in_proj(x, g, w_main, w_gate, conv_w, prm):
    b, t, _ = x.shape
    tm = _token_block(t)
    tok = lambda d: pl.BlockSpec((None, tm, d), lambda i, j: (i, j, 0))
    return pl.pallas_call(
        functools.partial(_in_proj_kernel, tm=tm),
        grid=(b, t // tm),
        in_specs=_halo_specs(tm, t, D_MODEL) + [
            _const_spec((1, D_MODEL)),
            _const_spec((D_MODEL, D_MAIN)),
            _const_spec((D_MODEL, LANES)),
            _const_spec((3, 3 * D_DN)),
            _const_spec((2, LANES)),
        ],
        out_specs=[tok(3 * D_NA), tok(3 * D_DN), tok(D_DN), tok(LANES)],
        out_shape=[
            jax.ShapeDtypeStruct((b, t, 3 * D_NA), BF16),
            jax.ShapeDtypeStruct((b, t, 3 * D_DN), BF16),
            jax.ShapeDtypeStruct((b, t, D_DN), BF16),
            jax.ShapeDtypeStruct((b, t, LANES), F32),
        ],
        compiler_params=_params(("parallel", "parallel")),
        name="in_proj",
    )(x, x, x, g, w_main, w_gate, conv_w, prm)


def _na_bias_tables(rpb_l):
    c = np.arange(GRID_W)
    cs = np.clip(c - WIN_W // 2, 0, GRID_W - WIN_W)
    valid = (c[None, :] >= cs[:, None]) & (c[None, :] < cs[:, None] + WIN_W)
    coff = np.clip(c[None, :] - c[:, None] + (WIN_W - 1), 0, 2 * WIN_W - 2)
    roff = np.arange(WIN_H)[None, :] + (WIN_H - 1) - np.arange(WIN_H)[:, None]
    b = rpb_l[:, roff][:, :, :, coff]
    b = jnp.where(valid[None, None, None], b, -jnp.inf)
    b = jnp.transpose(b, (1, 0, 3, 2, 4))
    return b.reshape(WIN_H, N_PAIRS, 2 * GRID_W, WIN_H * GRID_W).astype(F32)


def _na_kernel(q_ref, k_ref, v_ref, bias_ref, o_ref, *, rows):
    step = pl.program_id(1)
    nk = WIN_H * GRID_W
    lane = lax.broadcasted_iota(jnp.int32, (GRID_W, LANES), 1)
    low = lane < HEAD_DIM

    problems = []
    for rr in range(NA_ROWS):
        r = step * NA_ROWS + rr
        rs = jnp.clip(r - WIN_H // 2, 0, rows - WIN_H)
        start = pl.multiple_of(rs * GRID_W, GRID_W)
        for p in range(N_PAIRS):
            problems.append((rr, p, r - rs, start))

    for g0 in range(0, len(problems), NA_LOCKSTEP):
        group = problems[g0:g0 + NA_LOCKSTEP]
        scores = []
        for rr, p, var, start in group:
            cols = slice(p * LANES, (p + 1) * LANES)
            q2 = q_ref[rr * GRID_W:(rr + 1) * GRID_W, cols].astype(F32)
            qs = jnp.concatenate([jnp.where(low, q2, 0.0), jnp.where(low, 0.0, q2)], axis=0).astype(BF16)
            scores.append(_dot_nt(qs, k_ref[pl.ds(start, nk), cols]) + bias_ref[var, p])
        probs = []
        for s in scores:
            e = jnp.exp(s - jnp.max(s, axis=-1, keepdims=True))
            probs.append((e.astype(BF16), jnp.sum(e, axis=-1, keepdims=True)))
        for (rr, p, var, start), (e, l) in zip(group, probs):
            cols = slice(p * LANES, (p + 1) * LANES)
            o = _dot(e, v_ref[pl.ds(start, nk), cols]) * (1.0 / l)
            o_ref[rr * GRID_W:(rr + 1) * GRID_W, cols] = jnp.where(low, o[:GRID_W], o[GRID_W:]).astype(o_ref.dtype)


def _na(qkv, bias, b, t):
    rows = t // GRID_W
    blk = NA_ROWS * GRID_W
    return pl.pallas_call(
        functools.partial(_na_kernel, rows=rows),
        grid=(b, rows // NA_ROWS),
        in_specs=[
            pl.BlockSpec((None, blk, D_NA), lambda i, r: (i, r, 0)),
            pl.BlockSpec((None, t, D_NA), lambda i, r: (i, 0, 1)),
            pl.BlockSpec((None, t, D_NA), lambda i, r: (i, 0, 2)),
            _const_spec((WIN_H, N_PAIRS, 2 * GRID_W, WIN_H * GRID_W)),
        ],
        out_specs=pl.BlockSpec((None, blk, D_NA), lambda i, r: (i, r, 0)),
        out_shape=jax.ShapeDtypeStruct((b, t, D_NA), BF16),
        compiler_params=_params(("parallel", "arbitrary")),
        name="na",
    )(qkv, qkv, qkv, bias)


def _block_diag(x, low):
    return jnp.concatenate([jnp.where(low, x, 0.0), jnp.where(low, 0.0, x)], axis=0)


def _seg_sum(x, ones_bd):
    hi, lo = _split2(x)
    return _dot(hi, ones_bd) + _dot(lo, ones_bd)


def _mm(a, b):
    return _dot(a.astype(BF16), b.astype(BF16))


def _neumann(nmats, low, eye2):
    ps = [_mm(nm, _block_diag(nm, low)) for nm in nmats]
    ts = [eye2 + nm for nm in nmats]
    for _ in range(4):
        outs = [_mm(p, jnp.concatenate([_block_diag(p, low), _block_diag(tt, low)], axis=1)) for p, tt in zip(ps, ts)]
        ps = [o[:, :LANES] for o in outs]
        ts = [tt + o[:, LANES:] for tt, o in zip(ts, outs)]
    return [tt + _mm(p, _block_diag(tt, low)) for p, tt in zip(ps, ts)]


def _dn_kernel(q_ref, k_ref, v_ref, z_ref, gate_ref, normo_ref, o_ref, m_s, n_s, p_s, r_s, gl_s, st_s, *, t):
    n = t // CHUNK
    grp_rows = DN_GROUP * CHUNK
    pair = pl.program_id(1)
    ones_bd = _head_block_ones(LANES)
    r128 = lax.broadcasted_iota(jnp.int32, (LANES, LANES), 0)
    c128 = lax.broadcasted_iota(jnp.int32, (LANES, LANES), 1)
    diag_blocks = (r128 // HEAD_DIM) == (c128 // HEAD_DIM)

    er = lax.broadcasted_iota(jnp.int32, (LANES, 4 * LANES), 0)
    ec = lax.broadcasted_iota(jnp.int32, (LANES, 4 * LANES), 1)
    src = (ec // LANES) * N_HEADS_DN + 2 * pair + (ec % LANES) // HEAD_DIM
    spread = (er == src).astype(BF16)
    spread2 = jnp.concatenate([spread, spread], axis=0)

    lane = lax.broadcasted_iota(jnp.int32, (CHUNK, LANES), 1)
    low = lane < HEAD_DIM
    ri = lax.broadcasted_iota(jnp.int32, (CHUNK, LANES), 0)
    ci = lane % HEAD_DIM
    eye2 = (ri == ci).astype(F32)
    ti = lax.broadcasted_iota(jnp.int32, (CHUNK, CHUNK), 0)
    tj = lax.broadcasted_iota(jnp.int32, (CHUNK, CHUNK), 1)
    neg_ones = jnp.full((CHUNK, CHUNK), -1.0, F32)

    dirs = []
    for tri, keep, upto, strict, last in (
            ((tj <= ti), ci <= ri, ci >= ri, ci < ri, CHUNK - 1),
            ((tj >= ti), ci >= ri, ci <= ri, ci > ri, 0)):
        trif = tri.astype(F32)
        lhs = jnp.concatenate([trif, neg_ones, trif, neg_ones], axis=1).astype(BF16)
        dirs.append((lhs, keep, upto, strict, last))

    def phase_a(grp, carry):
        r0 = pl.multiple_of(grp * grp_rows, grp_rows)
        rows_g = pl.ds(r0, grp_rows)
        kb = k_ref[rows_g, :]
        kf = kb.astype(F32)
        qf = q_ref[rows_g, :].astype(F32)
        vf = v_ref[rows_g, :].astype(F32)
        ghi, glo = _split2(gate_ref[rows_g, :])
        sp = _dot(jnp.concatenate([ghi, glo], axis=1), spread2)

        chunks = []
        for u in range(DN_GROUP):
            sl = slice(u * CHUNK, (u + 1) * CHUNK)
            kc, qc, vc = kf[sl], qf[sl], vf[sl]
            kq = jnp.concatenate([kc, qc], axis=0).astype(BF16)
            gq = _dot_nt(kq, _block_diag(kc, low).astype(BF16))
            chunks.append((kc, qc, vc, gq[:CHUNK], gq[CHUNK:], sp[sl]))

        dgs = []
        for kc, qc, vc, gram, qk, spc in chunks:
            for d, (lhs, keep, upto, strict, last) in enumerate(dirs):
                ghi2, glo2 = _split2(spc[:, (2 + d) * LANES:(3 + d) * LANES])
                zero = jnp.zeros_like(ghi2)
                rhs = jnp.concatenate([
                    jnp.concatenate([ghi2, jnp.where(upto, ghi2, zero), glo2, jnp.where(upto, glo2, zero)], axis=0),
                    jnp.concatenate([ghi2, zero, glo2, zero], axis=0)], axis=1)
                dgs.append(_dot(lhs, rhs))

        probs = []
        nmats = []
        it = iter(dgs)
        for kc, qc, vc, gram, qk, spc in chunks:
            for d, (lhs, keep, upto, strict, last) in enumerate(dirs):
                dg = next(it)
                beta = spc[:, d * LANES:(d + 1) * LANES]
                delta = dg[:, :LANES]
                gcol = dg[:, LANES:]
                decay = jnp.where(keep, jnp.exp(jnp.minimum(delta, 0.0)), 0.0)
                gamma = jnp.exp(gcol)
                glast = gcol[last:last + 1, :]
                nmats.append(jnp.where(strict, -(beta * gram * decay), 0.0))
                kd = (kc * jnp.exp(glast - gcol)).astype(BF16)
                probs.append((kc, qc, vc, beta, gamma, glast, qk * decay, kd))
        tinvs = _neumann(nmats, low, eye2)

        uws = []
        for (kc, qc, vc, beta, gamma, glast, qkd, kd), tinv in zip(probs, tinvs):
            rhs = jnp.concatenate([_block_diag(kc * (beta * gamma), low), _block_diag(vc * beta, low)], axis=1)
            uws.append(_mm(tinv, rhs))

        mns = []
        prs = []
        for (kc, qc, vc, beta, gamma, glast, qkd, kd), wu in zip(probs, uws):
            wub = wu.astype(BF16)
            mns.append(_dot_tn(kd, wub))
            rhs = jnp.concatenate([_block_diag(wu[:, :LANES], low), _block_diag(wu[:, LANES:], low)], axis=1)
            prs.append(_mm(qkd, rhs))

        idx = 0
        for u in range(DN_GROUP):
            c = grp * DN_GROUP + u
            rows = pl.ds(pl.multiple_of(r0 + u * CHUNK, CHUNK), CHUNK)
            for d in range(2):
                kc, qc, vc, beta, gamma, glast, qkd, kd = probs[idx]
                mn, pr = mns[idx], prs[idx]
                m_s[d, c] = jnp.where(diag_blocks, mn[:, :LANES], 0.0).astype(BF16)
                n_s[d, c] = jnp.where(diag_blocks, mn[:, LANES:], 0.0).astype(BF16)
                p_s[d, rows, :] = (qc * gamma - pr[:, :LANES]).astype(BF16)
                r_s[d, rows, :] = pr[:, LANES:]
                gl_s[d, c] = jnp.broadcast_to(jnp.exp(glast), (8, LANES))
                idx += 1
        return carry

    lax.fori_loop(0, n // DN_GROUP, phase_a, 0)

    def phase_b(i, carry):
        new = []
        for d, (c, state) in enumerate(zip((i, n - 1 - i), carry)):
            sb = state.astype(BF16)
            st_s[d, c] = sb
            new.append(state * gl_s[d, c][0:1, :] + n_s[d, c].astype(F32) - _dot(m_s[d, c], sb))
        return tuple(new)

    zero_state = jnp.zeros((LANES, LANES), F32)
    lax.fori_loop(0, n, phase_b, (zero_state, zero_state))

    def phase_c(grp, carry):
        r0 = pl.multiple_of(grp * grp_rows, grp_rows)
        outs = []
        for u in range(DN_GROUP):
            c = grp * DN_GROUP + u
            rows = pl.ds(pl.multiple_of(r0 + u * CHUNK, CHUNK), CHUNK)
            of = _dot(p_s[0, rows, :], st_s[0, c]) + r_s[0, rows, :]
            ob = _dot(p_s[1, rows, :], st_s[1, c]) + r_s[1, rows, :]
            outs.append(of + ob)
        o = jnp.concatenate(outs, axis=0)
        rows_g = pl.ds(r0, grp_rows)
        ms = _seg_sum(o * o, ones_bd) * (1.0 / HEAD_DIM)
        o = o * lax.rsqrt(ms + RMS_EPS) * normo_ref[...]
        z = z_ref[rows_g, :].astype(F32)
        o_ref[rows_g, :] = (o * (z * _sigmoid(z))).astype(o_ref.dtype)
        return carry

    lax.fori_loop(0, n // DN_GROUP, phase_c, 0)


def _dn(qkv, z, gates, normo2, b, t):
    n = t // CHUNK

    def col_spec(base):
        return pl.BlockSpec((None, t, LANES), lambda i, p: (i, 0, base + p))

    return pl.pallas_call(
        functools.partial(_dn_kernel, t=t),
        grid=(b, N_PAIRS),
        in_specs=[
            col_spec(0), col_spec(N_PAIRS), col_spec(2 * N_PAIRS),
            col_spec(0),
            pl.BlockSpec((None, t, LANES), lambda i, p: (i, 0, 0)),
            _const_spec((1, LANES)),
        ],
        out_specs=pl.BlockSpec((None, t, LANES), lambda i, p: (i, 0, p)),
        out_shape=jax.ShapeDtypeStruct((b, t, D_DN), BF16),
        scratch_shapes=[
            pltpu.VMEM((2, n, LANES, LANES), BF16),
            pltpu.VMEM((2, n, LANES, LANES), BF16),
            pltpu.VMEM((2, t, LANES), BF16),
            pltpu.VMEM((2, t, LANES), F32),
            pltpu.VMEM((2, n, 8, LANES), F32),
            pltpu.VMEM((2, n, LANES, LANES), BF16),
        ],
        compiler_params=_params(("parallel", "arbitrary")),
        name="dn",
    )(qkv, qkv, qkv, z, gates, normo2)


def _mem_kv_kernel(m_ref, g_ref, w_ref, k_ref, v_ref):
    mn = _rms(m_ref[...], g_ref[...]).astype(BF16)
    k_ref[...] = _dot(mn, w_ref[:, :D_MODEL]).astype(k_ref.dtype)
    v_ref[...] = _dot(mn, w_ref[:, D_MODEL:]).astype(v_ref.dtype)


def _mem_kv(mem, g, w_kv):
    b = mem.shape[0]
    blk = pl.BlockSpec((None, N_MEM, D_MODEL), lambda i: (i, 0, 0))
    return pl.pallas_call(
        _mem_kv_kernel,
        grid=(b,),
        in_specs=[blk, _const_spec((1, D_MODEL)), _const_spec((D_MODEL, 2 * D_MODEL))],
        out_specs=[blk, blk],
        out_shape=[jax.ShapeDtypeStruct((b, N_MEM, D_MODEL), BF16)] * 2,
        compiler_params=_params(("parallel",)),
        name="mem_kv",
    )(mem, g, w_kv)


def _mix_xattn_kernel(x_ref, yna_ref, ydn_ref, wout_ref, g_ref, wq_ref, k_ref, v_ref, wo_ref, o_ref):
    x = x_ref[...] + _dot(yna_ref[...], wout_ref[:D_NA, :]) + _dot(ydn_ref[...], wout_ref[D_NA:, :])
    xn = _rms(x, g_ref[...]).astype(BF16)
    q = (_dot(xn, wq_ref[...]) * (HEAD_DIM_X ** -0.5)).astype(BF16)
    heads = []
    for h in range(N_HEADS_X):
        cols = slice(h * HEAD_DIM_X, (h + 1) * HEAD_DIM_X)
        s = _dot_nt(q[:, cols], k_ref[:, cols])
        m = jnp.max(s, axis=-1, keepdims=True)
        e = jnp.exp(s - m)
        l = jnp.sum(e, axis=-1, keepdims=True)
        heads.append((_dot(e.astype(BF16), v_ref[:, cols]) * (1.0 / l)).astype(BF16))
    o_ref[...] = x + _dot(jnp.concatenate(heads, axis=-1), wo_ref[...])


def _mix_xattn(x, y_na, y_dn, w_out, g, w_q, kmem, vmem, w_o):
    b, t, _ = x.shape
    tm = _token_block(t)
    tok = lambda d: pl.BlockSpec((None, tm, d), lambda i, j: (i, j, 0))
    memspec = pl.BlockSpec((None, N_MEM, D_MODEL), lambda i, j: (i, 0, 0))
    return pl.pallas_call(
        _mix_xattn_kernel,
        grid=(b, t // tm),
        in_specs=[
            tok(D_MODEL), tok(D_NA), tok(D_DN),
            _const_spec((D_NA + D_DN, D_MODEL)),
            _const_spec((1, D_MODEL)),
            _const_spec((D_MODEL, D_MODEL)),
            memspec, memspec,
            _const_spec((D_MODEL, D_MODEL)),
        ],
        out_specs=tok(D_MODEL),
        out_shape=jax.ShapeDtypeStruct((b, t, D_MODEL), F32),
        compiler_params=_params(("parallel", "parallel")),
        name="mix_xattn",
    )(x, y_na, y_dn, w_out, g, w_q, kmem, vmem, w_o)


def _ffn_kernel(x_ref, xp_ref, xnx_ref, g_ref, wv_ref, wg_ref, cv_ref, cg_ref, bv_ref, bg_ref, wd_ref, gf_ref, o_ref,
                *, tm, final):
    xe = _normed_with_halo(x_ref, xp_ref, xnx_ref, g_ref[...])
    acc = x_ref[...]
    for c in range(N_FF_CHUNKS):
        val = _token_conv3(_dot(xe, wv_ref[c]), cv_ref[c], tm) + bv_ref[c]
        gate = _token_conv3(_dot(xe, wg_ref[c]), cg_ref[c], tm) + bg_ref[c]
        act = (gate * _sigmoid(gate) * val).astype(BF16)
        acc = acc + _dot(act, wd_ref[c])
    if final:
        acc = _rms(acc, gf_ref[...])
    o_ref[...] = acc


def _ffn(x, g, wv, wg, cv, cg, bv, bg, wd, g_final, final):
    b, t, _ = x.shape
    tm = _token_block(t)
    return pl.pallas_call(
        functools.partial(_ffn_kernel, tm=tm, final=final),
        grid=(b, t // tm),
        in_specs=_halo_specs(tm, t, D_MODEL) + [
            _const_spec((1, D_MODEL)),
            _const_spec((N_FF_CHUNKS, D_MODEL, FF_CHUNK)),
            _const_spec((N_FF_CHUNKS, D_MODEL, FF_CHUNK)),
            _const_spec((N_FF_CHUNKS, 3, FF_CHUNK)),
            _const_spec((N_FF_CHUNKS, 3, FF_CHUNK)),
            _const_spec((N_FF_CHUNKS, 1, FF_CHUNK)),
            _const_spec((N_FF_CHUNKS, 1, FF_CHUNK)),
            _const_spec((N_FF_CHUNKS, FF_CHUNK, D_MODEL)),
            _const_spec((1, D_MODEL)),
        ],
        out_specs=pl.BlockSpec((None, tm, D_MODEL), lambda i, j: (i, j, 0)),
        out_shape=jax.ShapeDtypeStruct((b, t, D_MODEL), F32),
        compiler_params=_params(("parallel", "parallel")),
        name="ffn",
    )(x, x, x, g, wv, wg, cv, cg, bv, bg, wd, g_final)


def _chunk_cols(w):
    rows = w.shape[0]
    parts = w.reshape(rows, 2, N_FF_CHUNKS, FF_CHUNK)
    return jnp.transpose(parts[:, 0], (1, 0, 2)), jnp.transpose(parts[:, 1], (1, 0, 2))


def _prep_layer(l, norm_mix, w_in, rpb, conv_qkv, a_log, dt_bias, norm_o, w_out, norm_x, norm_mem, w_xq, w_xkv, w_xo,
                norm_ffn, w_up, conv_ffn, conv_ffn_b, w_down):
    row = lambda v: v.reshape(1, -1).astype(F32)
    w_gate = jnp.pad(w_in[l][:, D_MAIN:], ((0, 0), (0, LANES - N_GATE))).astype(BF16)
    pad = jnp.zeros((2 * N_HEADS_DN,), F32)
    tail = jnp.zeros((LANES - N_GATE,), F32)
    prm = jnp.stack([jnp.concatenate([pad, a_log[l].reshape(-1), tail]),
                     jnp.concatenate([pad, dt_bias[l].reshape(-1), tail])])
    wv, wg = _chunk_cols(w_up[l].astype(BF16))
    cv, cg = _chunk_cols(conv_ffn[l])
    bv, bg = _chunk_cols(conv_ffn_b[l].reshape(1, -1))
    return dict(
        norm_mix=row(norm_mix[l]), w_main=w_in[l][:, :D_MAIN].astype(BF16), w_gate=w_gate,
        na_bias=_na_bias_tables(rpb[l]), conv_qkv=conv_qkv[l], prm=prm,
        norm_o=row(jnp.concatenate([norm_o[l], norm_o[l]])), w_out=w_out[l].astype(BF16),
        norm_x=row(norm_x[l]), norm_mem=row(norm_mem[l]), w_xq=w_xq[l].astype(BF16), w_xkv=w_xkv[l].astype(BF16),
        w_xo=w_xo[l].astype(BF16), norm_ffn=row(norm_ffn[l]), wv=wv, wg=wg, cv=cv, cg=cg, bv=bv, bg=bg,
        wd=w_down[l].astype(BF16).reshape(N_FF_CHUNKS, FF_CHUNK, D_MODEL))


def _layer(x, mem, p, g_final, final):
    b, t, _ = x.shape
    qkv_na, qkv_dn, z, gates = _in_proj(x, p["norm_mix"], p["w_main"], p["w_gate"], p["conv_qkv"], p["prm"])
    y_na = _na(qkv_na, p["na_bias"], b, t)
    y_dn = _dn(qkv_dn, z, gates, p["norm_o"], b, t)
    kmem, vmem = _mem_kv(mem, p["norm_mem"], p["w_xkv"])
    x = _mix_xattn(x, y_na, y_dn, p["w_out"], p["norm_x"], p["w_xq"], kmem, vmem, p["w_xo"])
    return _ffn(x, p["norm_ffn"], p["wv"], p["wg"], p["cv"], p["cg"], p["bv"], p["bg"], p["wd"], g_final, final)


def kernel(x_prompt, x_sample, mem_prompt, mem_sample, norm_mix, w_in, rpb, conv_qkv, a_log, dt_bias, norm_o, w_out,
           norm_x, norm_mem, w_xq, w_xkv, w_xo, norm_ffn, w_up, conv_ffn, conv_ffn_b, w_down, norm_final):
    layers = [_prep_layer(l, norm_mix, w_in, rpb, conv_qkv, a_log, dt_bias, norm_o, w_out, norm_x, norm_mem, w_xq,
                          w_xkv, w_xo, norm_ffn, w_up, conv_ffn, conv_ffn_b, w_down) for l in range(DEPTH)]
    g_final = norm_final.reshape(1, -1).astype(F32)
    outs = []
    for x, mem in ((x_prompt, mem_prompt), (x_sample, mem_sample)):
        for l in range(DEPTH):
            x = _layer(x, mem, layers[l], g_final, l == DEPTH - 1)
        outs.append(x)
    return tuple(outs)
```

```python
import functools

import numpy as np
import jax
import jax.numpy as jnp
from jax import lax
from jax.experimental import pallas as pl
from jax.experimental.pallas import tpu as pltpu

F32 = jnp.float32
BF16 = jnp.bfloat16

D_MODEL = 1024
DEPTH = 4
HEAD_DIM = 64
N_HEADS_NA = 8
N_HEADS_DN = 8
D_NA = N_HEADS_NA * HEAD_DIM
D_DN = N_HEADS_DN * HEAD_DIM
GRID_W = 64
WIN_H = 8
WIN_W = 16
CHUNK = 64
N_MEM = 256
N_HEADS_X = 4
HEAD_DIM_X = D_MODEL // N_HEADS_X
D_FF = 2816
RMS_EPS = 1e-6
L2_EPS = 1e-6

LANES = 128
MXU_DIM = 256
N_PAIRS = N_HEADS_DN // 2
D_MAIN = 3 * D_NA + 4 * D_DN
N_GATE = 4 * N_HEADS_DN
FF_CHUNK = 256
N_FF_CHUNKS = D_FF // FF_CHUNK
HALO = 8
NA_ROWS = 4
NA_LOCKSTEP = 8
DN_GROUP = 16
VMEM_LIMIT = 56 * 1024 * 1024


def _token_block(t):
    return min(512, t)


def _rms(x, g):
    return x * lax.rsqrt(jnp.mean(x * x, axis=-1, keepdims=True) + RMS_EPS) * g


def _sigmoid(x):
    return 1.0 / (1.0 + jnp.exp(-x))


def _split2(x):
    hi = x.astype(BF16)
    lo = (x - hi.astype(F32)).astype(BF16)
    return hi, lo


def _dot(a, b):
    return jnp.dot(a, b, preferred_element_type=F32)


def _dot_nt(a, b):
    return lax.dot_general(a, b, (((1,), (1,)), ((), ())), preferred_element_type=F32)


def _dot_tn(a, b):
    return lax.dot_general(a, b, (((0,), (0,)), ((), ())), preferred_element_type=F32)


def _head_block_ones(n):
    r = lax.broadcasted_iota(jnp.int32, (n, n), 0)
    c = lax.broadcasted_iota(jnp.int32, (n, n), 1)
    return ((r // HEAD_DIM) == (c // HEAD_DIM)).astype(BF16)


def _const_spec(shape):
    nd = len(shape)
    return pl.BlockSpec(shape, lambda *_: (0,) * nd)


def _params(sem):
    return pltpu.CompilerParams(dimension_semantics=sem, vmem_limit_bytes=VMEM_LIMIT)


def _halo_specs(tm, t, d):
    nh = tm // HALO
    last = t // HALO - 1
    return [
        pl.BlockSpec((None, tm, d), lambda i, j: (i, j, 0)),
        pl.BlockSpec((None, HALO, d), lambda i, j: (i, jnp.maximum(j * nh - 1, 0), 0)),
        pl.BlockSpec((None, HALO, d), lambda i, j: (i, jnp.minimum((j + 1) * nh, last), 0)),
    ]


def _normed_with_halo(x_ref, xp_ref, xnx_ref, g):
    j = pl.program_id(1)
    has_prev = jnp.where(j > 0, 1.0, 0.0)
    has_next = jnp.where(j < pl.num_programs(1) - 1, 1.0, 0.0)
    return jnp.concatenate([_rms(xp_ref[...], g) * has_prev, _rms(x_ref[...], g), _rms(xnx_ref[...], g) * has_next],
                           axis=0).astype(BF16)


def _token_conv3(h, w, tm):
    ext = tm + 2 * HALO
    hp = pltpu.roll(h, 1, 0)[HALO:HALO + tm]
    hn = pltpu.roll(h, ext - 1, 0)[HALO:HALO + tm]
    return hp * w[0:1] + h[HALO:HALO + tm] * w[1:2] + hn * w[2:3]


def _in_proj_kernel(x_ref, xp_ref, xnx_ref, g_ref, wm_ref, wg_ref, cw_ref, prm_ref, na_ref, dn_ref, z_ref, gate_ref,
                    *, tm):
    xe = _normed_with_halo(x_ref, xp_ref, xnx_ref, g_ref[...])
    xm = xe[HALO:HALO + tm]

    def na_part(j):
        r = _dot(xm, wm_ref[:, j * D_NA:(j + 1) * D_NA])
        if j == 0:
            r = r * (HEAD_DIM ** -0.5)
        na_ref[:, j * D_NA:(j + 1) * D_NA] = r.astype(na_ref.dtype)

    def z_part():
        off = 3 * D_NA + 3 * D_DN
        z_ref[...] = _dot(xm, wm_ref[:, off:off + D_DN]).astype(z_ref.dtype)

    def gate_part():
        raw = _dot(xm, wg_ref[...])
        col = lax.broadcasted_iota(jnp.int32, raw.shape, 1)
        xx = raw + prm_ref[1:2]
        softplus = jnp.maximum(xx, 0.0) + jnp.log(1.0 + jnp.exp(-jnp.abs(xx)))
        gate_ref[...] = jnp.where(col < 2 * N_HEADS_DN, _sigmoid(raw), -jnp.exp(prm_ref[0:1]) * softplus)

    fillers = [functools.partial(na_part, 0), functools.partial(na_part, 1), functools.partial(na_part, 2),
               z_part, gate_part]

    ones_bd = _head_block_ones(MXU_DIM)
    n_blocks = 3 * D_DN // MXU_DIM

    def dn_proj(c):
        off = 3 * D_NA + c * MXU_DIM
        return _dot(xe, wm_ref[:, off:off + MXU_DIM])

    h_next = dn_proj(0)
    for c in range(n_blocks):
        h = h_next
        if c + 1 < n_blocks:
            h_next = dn_proj(c + 1)
        if fillers:
            fillers.pop(0)()
        cols = slice(c * MXU_DIM, (c + 1) * MXU_DIM)
        y = _token_conv3(h, cw_ref[:, cols], tm)
        y = y * _sigmoid(y)
        if c < 2 * D_DN // MXU_DIM:
            y = y * lax.rsqrt(_dot((y * y).astype(BF16), ones_bd) + L2_EPS)
        if c < D_DN // MXU_DIM:
            y = y * (HEAD_DIM ** -0.5)
        dn_ref[:, cols] = y.astype(dn_ref.dtype)
    for f in fillers:
        f()


def _in_proj(x, g, w_main, w_gate, conv_w, prm):
    b, t, _ = x.shape
    tm = _token_block(t)
    tok = lambda d: pl.BlockSpec((None, tm, d), lambda i, j: (i, j, 0))
    return pl.pallas_call(
        functools.partial(_in_proj_kernel, tm=tm),
        grid=(b, t // tm),
        in_specs=_halo_specs(tm, t, D_MODEL) + [
            _const_spec((1, D_MODEL)),
            _const_spec((D_MODEL, D_MAIN)),
            _const_spec((D_MODEL, LANES)),
            _const_spec((3, 3 * D_DN)),
            _const_spec((2, LANES)),
        ],
        out_specs=[tok(3 * D_NA), tok(3 * D_DN), tok(D_DN), tok(LANES)],
        out_shape=[
            jax.ShapeDtypeStruct((b, t, 3 * D_NA), BF16),
            jax.ShapeDtypeStruct((b, t, 3 * D_DN), BF16),
            jax.ShapeDtypeStruct((b, t, D_DN), BF16),
            jax.ShapeDtypeStruct((b, t, LANES), F32),
        ],
        compiler_params=_params(("parallel", "parallel")),
        name="in_proj",
    )(x, x, x, g, w_main, w_gate, conv_w, prm)


def _na_bias_tables(rpb_l):
    c = np.arange(GRID_W)
    cs = np.clip(c - WIN_W // 2, 0, GRID_W - WIN_W)
    valid = (c[None, :] >= cs[:, None]) & (c[None, :] < cs[:, None] + WIN_W)
    coff = np.clip(c[None, :] - c[:, None] + (WIN_W - 1), 0, 2 * WIN_W - 2)
    roff = np.arange(WIN_H)[None, :] + (WIN_H - 1) - np.arange(WIN_H)[:, None]
    b = rpb_l[:, roff][:, :, :, coff]
    b = jnp.where(valid[None, None, None], b, -jnp.inf)
    b = jnp.transpose(b, (1, 0, 3, 2, 4))
    return b.reshape(WIN_H, N_PAIRS, 2 * GRID_W, WIN_H * GRID_W).astype(F32)


def _na_kernel(q_ref, k_ref, v_ref, bias_ref, o_ref, *, rows):
    step = pl.program_id(1)
    nk = WIN_H * GRID_W
    lane = lax.broadcasted_iota(jnp.int32, (GRID_W, LANES), 1)
    low = lane < HEAD_DIM

    problems = []
    for rr in range(NA_ROWS):
        r = step * NA_ROWS + rr
        rs = jnp.clip(r - WIN_H // 2, 0, rows - WIN_H)
        start = pl.multiple_of(rs * GRID_W, GRID_W)
        for p in range(N_PAIRS):
            problems.append((rr, p, r - rs, start))

    for g0 in range(0, len(problems), NA_LOCKSTEP):
        group = problems[g0:g0 + NA_LOCKSTEP]
        scores = []
        for rr, p, var, start in group:
            cols = slice(p * LANES, (p + 1) * LANES)
            q2 = q_ref[rr * GRID_W:(rr + 1) * GRID_W, cols].astype(F32)
            qs = jnp.concatenate([jnp.where(low, q2, 0.0), jnp.where(low, 0.0, q2)], axis=0).astype(BF16)
            scores.append(_dot_nt(qs, k_ref[pl.ds(start, nk), cols]) + bias_ref[var, p])
        probs = []
        for s in scores:
            e = jnp.exp(s - jnp.max(s, axis=-1, keepdims=True))
            probs.append((e.astype(BF16), jnp.sum(e, axis=-1, keepdims=True)))
        for (rr, p, var, start), (e, l) in zip(group, probs):
            cols = slice(p * LANES, (p + 1) * LANES)
            o = _dot(e, v_ref[pl.ds(start, nk), cols]) * (1.0 / l)
            o_ref[rr * GRID_W:(rr + 1) * GRID_W, cols] = jnp.where(low, o[:GRID_W], o[GRID_W:]).astype(o_ref.dtype)


def _na(qkv, bias, b, t):
    rows = t // GRID_W
    blk = NA_ROWS * GRID_W
    return pl.pallas_call(
        functools.partial(_na_kernel, rows=rows),
        grid=(b, rows // NA_ROWS),
        in_specs=[
            pl.BlockSpec((None, blk, D_NA), lambda i, r: (i, r, 0)),
            pl.BlockSpec((None, t, D_NA), lambda i, r: (i, 0, 1)),
            pl.BlockSpec((None, t, D_NA), lambda i, r: (i, 0, 2)),
            _const_spec((WIN_H, N_PAIRS, 2 * GRID_W, WIN_H * GRID_W)),
        ],
        out_specs=pl.BlockSpec((None, blk, D_NA), lambda i, r: (i, r, 0)),
        out_shape=jax.ShapeDtypeStruct((b, t, D_NA), BF16),
        compiler_params=_params(("parallel", "arbitrary")),
        name="na",
    )(qkv, qkv, qkv, bias)


def _block_diag(x, low):
    return jnp.concatenate([jnp.where(low, x, 0.0), jnp.where(low, 0.0, x)], axis=0)


def _diag_blocks(x, low):
    return jnp.where(low, x[:HEAD_DIM], x[HEAD_DIM:])


def _seg_sum(x, ones_bd):
    hi, lo = _split2(x)
    return _dot(hi, ones_bd) + _dot(lo, ones_bd)


def _mm(a, b):
    return _dot(a.astype(BF16), b.astype(BF16))


def _neumann(nmats, low, eye2, between):
    ps = [_mm(nm, _block_diag(nm, low)) for nm in nmats]
    ts = [eye2 + nm for nm in nmats]
    between()
    for _ in range(4):
        outs = [_mm(p, jnp.concatenate([_block_diag(p, low), _block_diag(tt, low)], axis=1)) for p, tt in zip(ps, ts)]
        ps = [o[:, :LANES] for o in outs]
        ts = [tt + o[:, LANES:] for tt, o in zip(ts, outs)]
        between()
    res = [tt + _mm(p, _block_diag(tt, low)) for p, tt in zip(ps, ts)]
    between()
    return res


def _dn_kernel(q_ref, k_ref, v_ref, gate_ref, z_ref, normo_ref, o_ref, m_s, n_s, p_s, r_s, gl_s, out_s,
               *, t, n_items):
    n = t // CHUNK
    group = min(DN_GROUP, n)
    grp_rows = group * CHUNK
    step = pl.program_id(0)
    cur = step % 2
    prev = 1 - cur
    pair = jnp.minimum(step, n_items - 1) % N_PAIRS
    ones_bd = _head_block_ones(LANES)

    @pl.when(step == 0)
    def _():
        m_s[1] = jnp.zeros(m_s.shape[1:], m_s.dtype)
        n_s[1] = jnp.zeros(n_s.shape[1:], n_s.dtype)
        p_s[1] = jnp.zeros(p_s.shape[1:], p_s.dtype)
        r_s[1] = jnp.zeros(r_s.shape[1:], r_s.dtype)
        gl_s[1] = jnp.zeros(gl_s.shape[1:], gl_s.dtype)

    er = lax.broadcasted_iota(jnp.int32, (LANES, 4 * LANES), 0)
    ec = lax.broadcasted_iota(jnp.int32, (LANES, 4 * LANES), 1)
    src = (ec // LANES) * N_HEADS_DN + 2 * pair + (ec % LANES) // HEAD_DIM
    spread = (er == src).astype(BF16)
    spread2 = jnp.concatenate([spread, spread], axis=0)

    lane = lax.broadcasted_iota(jnp.int32, (CHUNK, LANES), 1)
    low = lane < HEAD_DIM
    ri = lax.broadcasted_iota(jnp.int32, (CHUNK, LANES), 0)
    ci = lane % HEAD_DIM
    eye2 = (ri == ci).astype(F32)
    ti = lax.broadcasted_iota(jnp.int32, (CHUNK, CHUNK), 0)
    tj = lax.broadcasted_iota(jnp.int32, (CHUNK, CHUNK), 1)
    neg_ones = jnp.full((CHUNK, CHUNK), -1.0, F32)

    dirs = []
    for tri, keep, upto, strict, last in (
            ((tj <= ti), ci <= ri, ci >= ri, ci < ri, CHUNK - 1),
            ((tj >= ti), ci >= ri, ci <= ri, ci > ri, 0)):
        trif = tri.astype(F32)
        lhs = jnp.concatenate([trif, neg_ones, trif, neg_ones], axis=1).astype(BF16)
        dirs.append((lhs, keep, upto, strict, last))

    def body(grp, states):
        states = list(states)
        done = [0]

        def recurrence_steps(count):
            for _ in range(count):
                k = grp * group + done[0]
                done[0] += 1
                for d, c in enumerate((k, n - 1 - k)):
                    rows = pl.ds(pl.multiple_of(c * CHUNK, CHUNK), CHUNK)
                    s2 = states[d]
                    lhs = jnp.concatenate([m_s[prev, d, c], p_s[prev, d, rows, :]], axis=0)
                    ms_ps = _dot(lhs, _block_diag(s2, low).astype(BF16))
                    out_s[d, rows, :] = ms_ps[CHUNK:] + r_s[prev, d, rows, :]
                    states[d] = s2 * gl_s[prev, d, c][0:1, :] + n_s[prev, d, c].astype(F32) - ms_ps[:CHUNK]

        rounds = 10
        schedule = iter([group // rounds + (1 if i < group % rounds else 0) for i in range(rounds)])
        between = lambda: recurrence_steps(next(schedule))

        r0 = pl.multiple_of(grp * grp_rows, grp_rows)
        rows_g = pl.ds(r0, grp_rows)
        kf = k_ref[rows_g, :].astype(F32)
        qf = q_ref[rows_g, :].astype(F32)
        vf = v_ref[rows_g, :].astype(F32)
        ghi, glo = _split2(gate_ref[rows_g, :])
        sp = _dot(jnp.concatenate([ghi, glo], axis=1), spread2)

        chunks = []
        for u in range(group):
            sl = slice(u * CHUNK, (u + 1) * CHUNK)
            kc, qc, vc = kf[sl], qf[sl], vf[sl]
            kq = jnp.concatenate([kc, qc], axis=0).astype(BF16)
            gq = _dot_nt(kq, _block_diag(kc, low).astype(BF16))
            chunks.append((kc, qc, vc, gq[:CHUNK], gq[CHUNK:], sp[sl]))
        between()

        dgs = []
        for kc, qc, vc, gram, qk, spc in chunks:
            for d, (lhs, keep, upto, strict, last) in enumerate(dirs):
                ghi2, glo2 = _split2(spc[:, (2 + d) * LANES:(3 + d) * LANES])
                zero = jnp.zeros_like(ghi2)
                rhs = jnp.concatenate([
                    jnp.concatenate([ghi2, jnp.where(upto, ghi2, zero), glo2, jnp.where(upto, glo2, zero)], axis=0),
                    jnp.concatenate([ghi2, zero, glo2, zero], axis=0)], axis=1)
                dgs.append(_dot(lhs, rhs))
        between()

        probs = []
        nmats = []
        it = iter(dgs)
        for kc, qc, vc, gram, qk, spc in chunks:
            for d, (lhs, keep, upto, strict, last) in enumerate(dirs):
                dg = next(it)
                beta = spc[:, d * LANES:(d + 1) * LANES]
                delta = dg[:, :LANES]
                gcol = dg[:, LANES:]
                decay = jnp.where(keep, jnp.exp(jnp.minimum(delta, 0.0)), 0.0)
                gamma = jnp.exp(gcol)
                glast = gcol[last:last + 1, :]
                nmats.append(jnp.where(strict, -(beta * gram * decay), 0.0))
                kd = (kc * jnp.exp(glast - gcol)).astype(BF16)
                probs.append((kc, qc, vc, beta, gamma, glast, qk * decay, kd))
        tinvs = _neumann(nmats, low, eye2, between)

        uws = []
        for (kc, qc, vc, beta, gamma, glast, qkd, kd), tinv in zip(probs, tinvs):
            rhs = jnp.concatenate([_block_diag(kc * (beta * gamma), low), _block_diag(vc * beta, low)], axis=1)
            uws.append(_mm(tinv, rhs))
        between()

        mns = []
        prs = []
        for (kc, qc, vc, beta, gamma, glast, qkd, kd), wu in zip(probs, uws):
            mns.append(_dot_tn(kd, wu.astype(BF16)))
            rhs = jnp.concatenate([_block_diag(wu[:, :LANES], low), _block_diag(wu[:, LANES:], low)], axis=1)
            prs.append(_mm(qkd, rhs))
        between()

        idx = 0
        for u in range(group):
            c = grp * group + u
            rows = pl.ds(pl.multiple_of(r0 + u * CHUNK, CHUNK), CHUNK)
            for d in range(2):
                kc, qc, vc, beta, gamma, glast, qkd, kd = probs[idx]
                mn, pr = mns[idx], prs[idx]
                m_s[cur, d, c] = _diag_blocks(mn[:, :LANES], low).astype(BF16)
                n_s[cur, d, c] = _diag_blocks(mn[:, LANES:], low).astype(BF16)
                p_s[cur, d, rows, :] = (qc * gamma - pr[:, :LANES]).astype(BF16)
                r_s[cur, d, rows, :] = pr[:, LANES:]
                gl_s[cur, d, c] = jnp.broadcast_to(jnp.exp(glast), (8, LANES))
                idx += 1
        assert done[0] == group
        return tuple(states)

    zero_state = jnp.zeros((CHUNK, LANES), F32)
    lax.fori_loop(0, n // group, body, (zero_state, zero_state))

    o = out_s[0] + out_s[1]
    ms = _seg_sum(o * o, ones_bd) * (1.0 / HEAD_DIM)
    o = o * lax.rsqrt(ms + RMS_EPS) * normo_ref[...]
    z = z_ref[...].astype(F32)
    o_ref[...] = (o * (z * _sigmoid(z))).astype(o_ref.dtype)


def _dn(qkv, z, gates, normo2, b, t):
    n = t // CHUNK
    n_items = b * N_PAIRS

    def cur_spec(base):
        def index(s):
            i = jnp.minimum(s, n_items - 1)
            return (i // N_PAIRS, 0, base + i % N_PAIRS)
        return pl.BlockSpec((None, t, LANES), index)

    def lag_index(s):
        i = jnp.maximum(s - 1, 0)
        return (i // N_PAIRS, 0, i % N_PAIRS)

    return pl.pallas_call(
        functools.partial(_dn_kernel, t=t, n_items=n_items),
        grid=(n_items + 1,),
        in_specs=[
            cur_spec(0), cur_spec(N_PAIRS), cur_spec(2 * N_PAIRS),
            pl.BlockSpec((None, t, LANES), lambda s: (jnp.minimum(s, n_items - 1) // N_PAIRS, 0, 0)),
            pl.BlockSpec((None, t, LANES), lag_index),
            _const_spec((1, LANES)),
        ],
        out_specs=pl.BlockSpec((None, t, LANES), lag_index),
        out_shape=jax.ShapeDtypeStruct((b, t, D_DN), BF16),
        scratch_shapes=[
            pltpu.VMEM((2, 2, n, CHUNK, LANES), BF16),
            pltpu.VMEM((2, 2, n, CHUNK, LANES), BF16),
            pltpu.VMEM((2, 2, t, LANES), BF16),
            pltpu.VMEM((2, 2, t, LANES), F32),
            pltpu.VMEM((2, 2, n, 8, LANES), F32),
            pltpu.VMEM((2, t, LANES), F32),
        ],
        compiler_params=_params(("arbitrary",)),
        name="dn",
    )(qkv, qkv, qkv, gates, z, normo2)


def _mem_kv_kernel(m_ref, g_ref, w_ref, k_ref, v_ref):
    mn = _rms(m_ref[...], g_ref[...]).astype(BF16)
    k_ref[...] = _dot(mn, w_ref[:, :D_MODEL]).astype(k_ref.dtype)
    v_ref[...] = _dot(mn, w_ref[:, D_MODEL:]).astype(v_ref.dtype)


def _mem_kv(mem, g, w_kv):
    b = mem.shape[0]
    blk = pl.BlockSpec((None, N_MEM, D_MODEL), lambda i: (i, 0, 0))
    return pl.pallas_call(
        _mem_kv_kernel,
        grid=(b,),
        in_specs=[blk, _const_spec((1, D_MODEL)), _const_spec((D_MODEL, 2 * D_MODEL))],
        out_specs=[blk, blk],
        out_shape=[jax.ShapeDtypeStruct((b, N_MEM, D_MODEL), BF16)] * 2,
        compiler_params=_params(("parallel",)),
        name="mem_kv",
    )(mem, g, w_kv)


def _mix_xattn_kernel(x_ref, yna_ref, ydn_ref, wout_ref, g_ref, wq_ref, k_ref, v_ref, wo_ref, o_ref):
    x = x_ref[...] + _dot(yna_ref[...], wout_ref[:D_NA, :]) + _dot(ydn_ref[...], wout_ref[D_NA:, :])
    xn = _rms(x, g_ref[...]).astype(BF16)
    q = (_dot(xn, wq_ref[...]) * (HEAD_DIM_X ** -0.5)).astype(BF16)
    head_cols = [slice(h * HEAD_DIM_X, (h + 1) * HEAD_DIM_X) for h in range(N_HEADS_X)]
    scores = [_dot_nt(q[:, cols], k_ref[:, cols]) for cols in head_cols]
    probs = []
    for s in scores:
        e = jnp.exp(s - jnp.max(s, axis=-1, keepdims=True))
        probs.append((e.astype(BF16), jnp.sum(e, axis=-1, keepdims=True)))
    heads = [(_dot(e, v_ref[:, cols]) * (1.0 / l)).astype(BF16) for (e, l), cols in zip(probs, head_cols)]
    o_ref[...] = x + _dot(jnp.concatenate(heads, axis=-1), wo_ref[...])


def _mix_xattn(x, y_na, y_dn, w_out, g, w_q, kmem, vmem, w_o):
    b, t, _ = x.shape
    tm = _token_block(t)
    tok = lambda d: pl.BlockSpec((None, tm, d), lambda i, j: (i, j, 0))
    memspec = pl.BlockSpec((None, N_MEM, D_MODEL), lambda i, j: (i, 0, 0))
    return pl.pallas_call(
        _mix_xattn_kernel,
        grid=(b, t // tm),
        in_specs=[
            tok(D_MODEL), tok(D_NA), tok(D_DN),
            _const_spec((D_NA + D_DN, D_MODEL)),
            _const_spec((1, D_MODEL)),
            _const_spec((D_MODEL, D_MODEL)),
            memspec, memspec,
            _const_spec((D_MODEL, D_MODEL)),
        ],
        out_specs=tok(D_MODEL),
        out_shape=jax.ShapeDtypeStruct((b, t, D_MODEL), F32),
        compiler_params=_params(("parallel", "parallel")),
        name="mix_xattn",
    )(x, y_na, y_dn, w_out, g, w_q, kmem, vmem, w_o)


def _ffn_kernel(x_ref, xp_ref, xnx_ref, g_ref, wv_ref, wg_ref, cv_ref, cg_ref, bv_ref, bg_ref, wd_ref, gf_ref, o_ref,
                *, tm, final):
    xe = _normed_with_halo(x_ref, xp_ref, xnx_ref, g_ref[...])
    acc = x_ref[...]
    up = (_dot(xe, wv_ref[0]), _dot(xe, wg_ref[0]))
    for c in range(N_FF_CHUNKS):
        hv, hg = up
        if c + 1 < N_FF_CHUNKS:
            up = (_dot(xe, wv_ref[c + 1]), _dot(xe, wg_ref[c + 1]))
        val = _token_conv3(hv, cv_ref[c], tm) + bv_ref[c]
        gate = _token_conv3(hg, cg_ref[c], tm) + bg_ref[c]
        act = (gate * _sigmoid(gate) * val).astype(BF16)
        acc = acc + _dot(act, wd_ref[c])
    if final:
        acc = _rms(acc, gf_ref[...])
    o_ref[...] = acc


def _ffn(x, g, wv, wg, cv, cg, bv, bg, wd, g_final, final):
    b, t, _ = x.shape
    tm = _token_block(t)
    return pl.pallas_call(
        functools.partial(_ffn_kernel, tm=tm, final=final),
        grid=(b, t // tm),
        in_specs=_halo_specs(tm, t, D_MODEL) + [
            _const_spec((1, D_MODEL)),
            _const_spec((N_FF_CHUNKS, D_MODEL, FF_CHUNK)),
            _const_spec((N_FF_CHUNKS, D_MODEL, FF_CHUNK)),
            _const_spec((N_FF_CHUNKS, 3, FF_CHUNK)),
            _const_spec((N_FF_CHUNKS, 3, FF_CHUNK)),
            _const_spec((N_FF_CHUNKS, 1, FF_CHUNK)),
            _const_spec((N_FF_CHUNKS, 1, FF_CHUNK)),
            _const_spec((N_FF_CHUNKS, FF_CHUNK, D_MODEL)),
            _const_spec((1, D_MODEL)),
        ],
        out_specs=pl.BlockSpec((None, tm, D_MODEL), lambda i, j: (i, j, 0)),
        out_shape=jax.ShapeDtypeStruct((b, t, D_MODEL), F32),
        compiler_params=_params(("parallel", "parallel")),
        name="ffn",
    )(x, x, x, g, wv, wg, cv, cg, bv, bg, wd, g_final)


def _chunk_cols(w):
    rows = w.shape[0]
    parts = w.reshape(rows, 2, N_FF_CHUNKS, FF_CHUNK)
    return jnp.transpose(parts[:, 0], (1, 0, 2)), jnp.transpose(parts[:, 1], (1, 0, 2))


def _prep_layer(l, norm_mix, w_in, rpb, conv_qkv, a_log, dt_bias, norm_o, w_out, norm_x, norm_mem, w_xq, w_xkv, w_xo,
                norm_ffn, w_up, conv_ffn, conv_ffn_b, w_down):
    row = lambda v: v.reshape(1, -1).astype(F32)
    w_gate = jnp.pad(w_in[l][:, D_MAIN:], ((0, 0), (0, LANES - N_GATE))).astype(BF16)
    pad = jnp.zeros((2 * N_HEADS_DN,), F32)
    tail = jnp.zeros((LANES - N_GATE,), F32)
    prm = jnp.stack([jnp.concatenate([pad, a_log[l].reshape(-1), tail]),
                     jnp.concatenate([pad, dt_bias[l].reshape(-1), tail])])
    wv, wg = _chunk_cols(w_up[l].astype(BF16))
    cv, cg = _chunk_cols(conv_ffn[l])
    bv, bg = _chunk_cols(conv_ffn_b[l].reshape(1, -1))
    return dict(
        norm_mix=row(norm_mix[l]), w_main=w_in[l][:, :D_MAIN].astype(BF16), w_gate=w_gate,
        na_bias=_na_bias_tables(rpb[l]), conv_qkv=conv_qkv[l], prm=prm,
        norm_o=row(jnp.concatenate([norm_o[l], norm_o[l]])), w_out=w_out[l].astype(BF16),
        norm_x=row(norm_x[l]), norm_mem=row(norm_mem[l]), w_xq=w_xq[l].astype(BF16), w_xkv=w_xkv[l].astype(BF16),
        w_xo=w_xo[l].astype(BF16), norm_ffn=row(norm_ffn[l]), wv=wv, wg=wg, cv=cv, cg=cg, bv=bv, bg=bg,
        wd=w_down[l].astype(BF16).reshape(N_FF_CHUNKS, FF_CHUNK, D_MODEL))


def _layer(x, mem, p, g_final, final):
    b, t, _ = x.shape
    qkv_na, qkv_dn, z, gates = _in_proj(x, p["norm_mix"], p["w_main"], p["w_gate"], p["conv_qkv"], p["prm"])
    y_na = _na(qkv_na, p["na_bias"], b, t)
    y_dn = _dn(qkv_dn, z, gates, p["norm_o"], b, t)
    kmem, vmem = _mem_kv(mem, p["norm_mem"], p["w_xkv"])
    x = _mix_xattn(x, y_na, y_dn, p["w_out"], p["norm_x"], p["w_xq"], kmem, vmem, p["w_xo"])
    return _ffn(x, p["norm_ffn"], p["wv"], p["wg"], p["cv"], p["cg"], p["bv"], p["bg"], p["wd"], g_final, final)


def kernel(x_prompt, x_sample, mem_prompt, mem_sample, norm_mix, w_in, rpb, conv_qkv, a_log, dt_bias, norm_o, w_out,
           norm_x, norm_mem, w_xq, w_xkv, w_xo, norm_ffn, w_up, conv_ffn, conv_ffn_b, w_down, norm_final):
    layers = [_prep_layer(l, norm_mix, w_in, rpb, conv_qkv, a_log, dt_bias, norm_o, w_out, norm_x, norm_mem, w_xq,
                          w_xkv, w_xo, norm_ffn, w_up, conv_ffn, conv_ffn_b, w_down) for l in range(DEPTH)]
    g_final = norm_final.reshape(1, -1).astype(F32)
    outs = []
    for x, mem in ((x_prompt, mem_prompt), (x_sample, mem_sample)):
        for l in range(DEPTH):
            x = _layer(x, mem, layers[l], g_final, l == DEPTH - 1)
        outs.append(x)
    return tuple(outs)
```

```python
import functools

import numpy as np
import jax
import jax.numpy as jnp
from jax import lax
from jax.experimental import pallas as pl
from jax.experimental.pallas import tpu as pltpu

F32 = jnp.float32
BF16 = jnp.bfloat16

D_MODEL = 1024
DEPTH = 4
HEAD_DIM = 64
N_HEADS_NA = 8
N_HEADS_DN = 8
D_NA = N_HEADS_NA * HEAD_DIM
D_DN = N_HEADS_DN * HEAD_DIM
GRID_W = 64
WIN_H = 8
WIN_W = 16
CHUNK = 64
N_MEM = 256
N_HEADS_X = 4
HEAD_DIM_X = D_MODEL // N_HEADS_X
D_FF = 2816
RMS_EPS = 1e-6
L2_EPS = 1e-6

LANES = 128
MXU_DIM = 256
N_PAIRS = N_HEADS_DN // 2
D_MAIN = 3 * D_NA + 4 * D_DN
N_GATE = 4 * N_HEADS_DN
FF_CHUNK = 256
N_FF_CHUNKS = D_FF // FF_CHUNK
FFN_AHEAD = 2
HALO = 8
NA_ROWS = 8
NA_AHEAD = 4
DN_GROUP = 16
VMEM_LIMIT = 56 * 1024 * 1024


def _token_block(t):
    return min(1024, t)


def _rms(x, g):
    return x * lax.rsqrt(jnp.mean(x * x, axis=-1, keepdims=True) + RMS_EPS) * g


def _sigmoid(x):
    return 1.0 / (1.0 + jnp.exp(-x))


def _split2(x):
    hi = x.astype(BF16)
    lo = (x - hi.astype(F32)).astype(BF16)
    return hi, lo


def _dot(a, b):
    return jnp.dot(a, b, preferred_element_type=F32)


def _dot_nt(a, b):
    return lax.dot_general(a, b, (((1,), (1,)), ((), ())), preferred_element_type=F32)


def _dot_tn(a, b):
    return lax.dot_general(a, b, (((0,), (0,)), ((), ())), preferred_element_type=F32)


def _head_block_ones(n):
    r = lax.broadcasted_iota(jnp.int32, (n, n), 0)
    c = lax.broadcasted_iota(jnp.int32, (n, n), 1)
    return ((r // HEAD_DIM) == (c // HEAD_DIM)).astype(BF16)


def _const_spec(shape):
    nd = len(shape)
    return pl.BlockSpec(shape, lambda *_: (0,) * nd)


def _params(sem):
    return pltpu.CompilerParams(dimension_semantics=sem, vmem_limit_bytes=VMEM_LIMIT)


def _halo_specs(tm, t, d):
    nh = tm // HALO
    last = t // HALO - 1
    return [
        pl.BlockSpec((None, tm, d), lambda i, j: (i, j, 0)),
        pl.BlockSpec((None, HALO, d), lambda i, j: (i, jnp.maximum(j * nh - 1, 0), 0)),
        pl.BlockSpec((None, HALO, d), lambda i, j: (i, jnp.minimum((j + 1) * nh, last), 0)),
    ]


def _normed_with_halo(x_ref, xp_ref, xnx_ref, g):
    j = pl.program_id(1)
    has_prev = jnp.where(j > 0, 1.0, 0.0)
    has_next = jnp.where(j < pl.num_programs(1) - 1, 1.0, 0.0)
    return jnp.concatenate([_rms(xp_ref[...], g) * has_prev, _rms(x_ref[...], g), _rms(xnx_ref[...], g) * has_next],
                           axis=0).astype(BF16)


def _token_conv3(h, w, tm):
    ext = tm + 2 * HALO
    hp = pltpu.roll(h, 1, 0)[HALO:HALO + tm]
    hn = pltpu.roll(h, ext - 1, 0)[HALO:HALO + tm]
    return hp * w[0:1] + h[HALO:HALO + tm] * w[1:2] + hn * w[2:3]


def _in_proj_kernel(x_ref, xp_ref, xnx_ref, g_ref, wm_ref, wg_ref, cw_ref, prm_ref, na_ref, dn_ref, z_ref, gate_ref,
                    *, tm):
    xe = _normed_with_halo(x_ref, xp_ref, xnx_ref, g_ref[...])
    xm = xe[HALO:HALO + tm]

    def na_part(j):
        r = _dot(xm, wm_ref[:, j * D_NA:(j + 1) * D_NA])
        if j == 0:
            r = r * (HEAD_DIM ** -0.5)
        na_ref[:, j * D_NA:(j + 1) * D_NA] = r.astype(na_ref.dtype)

    def z_part():
        off = 3 * D_NA + 3 * D_DN
        z_ref[...] = _dot(xm, wm_ref[:, off:off + D_DN]).astype(z_ref.dtype)

    def gate_part():
        raw = _dot(xm, wg_ref[...])
        col = lax.broadcasted_iota(jnp.int32, raw.shape, 1)
        xx = raw + prm_ref[1:2]
        softplus = jnp.maximum(xx, 0.0) + jnp.log(1.0 + jnp.exp(-jnp.abs(xx)))
        gate_ref[...] = jnp.where(col < 2 * N_HEADS_DN, _sigmoid(raw), -jnp.exp(prm_ref[0:1]) * softplus)

    fillers = [functools.partial(na_part, 0), functools.partial(na_part, 1), functools.partial(na_part, 2),
               z_part, gate_part]

    ones_bd = _head_block_ones(MXU_DIM)
    n_blocks = 3 * D_DN // MXU_DIM

    def dn_proj(c):
        off = 3 * D_NA + c * MXU_DIM
        return _dot(xe, wm_ref[:, off:off + MXU_DIM])

    h_next = dn_proj(0)
    for c in range(n_blocks):
        h = h_next
        if c + 1 < n_blocks:
            h_next = dn_proj(c + 1)
        if fillers:
            fillers.pop(0)()
        cols = slice(c * MXU_DIM, (c + 1) * MXU_DIM)
        y = _token_conv3(h, cw_ref[:, cols], tm)
        y = y * _sigmoid(y)
        if c < 2 * D_DN // MXU_DIM:
            y = y * lax.rsqrt(_dot((y * y).astype(BF16), ones_bd) + L2_EPS)
        if c < D_DN // MXU_DIM:
            y = y * (HEAD_DIM ** -0.5)
        dn_ref[:, cols] = y.astype(dn_ref.dtype)
    for f in fillers:
        f()


def _in_proj(x, g, w_main, w_gate, conv_w, prm):
    b, t, _ = x.shape
    tm = _token_block(t)
    tok = lambda d: pl.BlockSpec((None, tm, d), lambda i, j: (i, j, 0))
    return pl.pallas_call(
        functools.partial(_in_proj_kernel, tm=tm),
        grid=(b, t // tm),
        in_specs=_halo_specs(tm, t, D_MODEL) + [
            _const_spec((1, D_MODEL)),
            _const_spec((D_MODEL, D_MAIN)),
            _const_spec((D_MODEL, LANES)),
            _const_spec((3, 3 * D_DN)),
            _const_spec((2, LANES)),
        ],
        out_specs=[tok(3 * D_NA), tok(3 * D_DN), tok(D_DN), tok(LANES)],
        out_shape=[
            jax.ShapeDtypeStruct((b, t, 3 * D_NA), BF16),
            jax.ShapeDtypeStruct((b, t, 3 * D_DN), BF16),
            jax.ShapeDtypeStruct((b, t, D_DN), BF16),
            jax.ShapeDtypeStruct((b, t, LANES), F32),
        ],
        compiler_params=_params(("parallel", "parallel")),
        name="in_proj",
    )(x, x, x, g, w_main, w_gate, conv_w, prm)


def _na_bias_tables(rpb_l):
    c = np.arange(GRID_W)
    cs = np.clip(c - WIN_W // 2, 0, GRID_W - WIN_W)
    valid = (c[None, :] >= cs[:, None]) & (c[None, :] < cs[:, None] + WIN_W)
    coff = np.clip(c[None, :] - c[:, None] + (WIN_W - 1), 0, 2 * WIN_W - 2)
    roff = np.arange(WIN_H)[None, :] + (WIN_H - 1) - np.arange(WIN_H)[:, None]
    b = rpb_l[:, roff][:, :, :, coff]
    b = jnp.where(valid[None, None, None], b, -jnp.inf)
    b = jnp.transpose(b, (1, 0, 3, 2, 4))
    return b.reshape(WIN_H, N_PAIRS, 2 * GRID_W, WIN_H * GRID_W).astype(F32)


def _na_kernel(q_ref, k_ref, v_ref, bias_ref, o_ref, *, rows):
    step = pl.program_id(1)
    nk = WIN_H * GRID_W
    lane = lax.broadcasted_iota(jnp.int32, (GRID_W, LANES), 1)
    low = lane < HEAD_DIM

    problems = []
    for rr in range(NA_ROWS):
        r = step * NA_ROWS + rr
        rs = jnp.clip(r - WIN_H // 2, 0, rows - WIN_H)
        start = pl.multiple_of(rs * GRID_W, GRID_W)
        for p in range(N_PAIRS):
            problems.append((rr, p, r - rs, start))

    def scores(rr, p, var, start):
        cols = slice(p * LANES, (p + 1) * LANES)
        q2 = q_ref[rr * GRID_W:(rr + 1) * GRID_W, cols].astype(F32)
        qs = jnp.concatenate([jnp.where(low, q2, 0.0), jnp.where(low, 0.0, q2)], axis=0).astype(BF16)
        return _dot_nt(qs, k_ref[pl.ds(start, nk), cols]) + bias_ref[var, p]

    pending = [scores(*prob) for prob in problems[:NA_AHEAD]]
    for i, (rr, p, var, start) in enumerate(problems):
        s = pending.pop(0)
        if i + NA_AHEAD < len(problems):
            pending.append(scores(*problems[i + NA_AHEAD]))
        e = jnp.exp(s - jnp.max(s, axis=-1, keepdims=True))
        l = jnp.sum(e, axis=-1, keepdims=True)
        cols = slice(p * LANES, (p + 1) * LANES)
        o = _dot(e.astype(BF16), v_ref[pl.ds(start, nk), cols]) * (1.0 / l)
        o_ref[rr * GRID_W:(rr + 1) * GRID_W, cols] = jnp.where(low, o[:GRID_W], o[GRID_W:]).astype(o_ref.dtype)


def _na(qkv, bias, b, t):
    rows = t // GRID_W
    blk = NA_ROWS * GRID_W
    return pl.pallas_call(
        functools.partial(_na_kernel, rows=rows),
        grid=(b, rows // NA_ROWS),
        in_specs=[
            pl.BlockSpec((None, blk, D_NA), lambda i, r: (i, r, 0)),
            pl.BlockSpec((None, t, D_NA), lambda i, r: (i, 0, 1)),
            pl.BlockSpec((None, t, D_NA), lambda i, r: (i, 0, 2)),
            _const_spec((WIN_H, N_PAIRS, 2 * GRID_W, WIN_H * GRID_W)),
        ],
        out_specs=pl.BlockSpec((None, blk, D_NA), lambda i, r: (i, r, 0)),
        out_shape=jax.ShapeDtypeStruct((b, t, D_NA), BF16),
        compiler_params=_params(("parallel", "arbitrary")),
        name="na",
    )(qkv, qkv, qkv, bias)


def _block_diag(x, low):
    return jnp.concatenate([jnp.where(low, x, 0.0), jnp.where(low, 0.0, x)], axis=0)


def _diag_blocks(x, low):
    return jnp.where(low, x[:HEAD_DIM], x[HEAD_DIM:])


def _seg_sum(x, ones_bd):
    hi, lo = _split2(x)
    return _dot(hi, ones_bd) + _dot(lo, ones_bd)


def _mm(a, b):
    return _dot(a.astype(BF16), b.astype(BF16))


def _neumann(nmats, low, eye2, between):
    ps = [_mm(nm, _block_diag(nm, low)) for nm in nmats]
    ts = [eye2 + nm for nm in nmats]
    between()
    for _ in range(4):
        outs = [_mm(p, jnp.concatenate([_block_diag(p, low), _block_diag(tt, low)], axis=1)) for p, tt in zip(ps, ts)]
        ps = [o[:, :LANES] for o in outs]
        ts = [tt + o[:, LANES:] for tt, o in zip(ts, outs)]
        between()
    res = [tt + _mm(p, _block_diag(tt, low)) for p, tt in zip(ps, ts)]
    between()
    return res


def _dn_kernel(q_ref, k_ref, v_ref, gate_ref, z_ref, normo_ref, o_ref, m_s, n_s, p_s, r_s, gl_s, out_s,
               *, t, n_items):
    n = t // CHUNK
    group = min(DN_GROUP, n)
    grp_rows = group * CHUNK
    step = pl.program_id(0)
    cur = step % 2
    prev = 1 - cur
    pair = jnp.minimum(step, n_items - 1) % N_PAIRS
    ones_bd = _head_block_ones(LANES)

    @pl.when(step == 0)
    def _():
        m_s[1] = jnp.zeros(m_s.shape[1:], m_s.dtype)
        n_s[1] = jnp.zeros(n_s.shape[1:], n_s.dtype)
        p_s[1] = jnp.zeros(p_s.shape[1:], p_s.dtype)
        r_s[1] = jnp.zeros(r_s.shape[1:], r_s.dtype)
        gl_s[1] = jnp.zeros(gl_s.shape[1:], gl_s.dtype)

    er = lax.broadcasted_iota(jnp.int32, (LANES, 4 * LANES), 0)
    ec = lax.broadcasted_iota(jnp.int32, (LANES, 4 * LANES), 1)
    src = (ec // LANES) * N_HEADS_DN + 2 * pair + (ec % LANES) // HEAD_DIM
    spread = (er == src).astype(BF16)
    spread2 = jnp.concatenate([spread, spread], axis=0)

    lane = lax.broadcasted_iota(jnp.int32, (CHUNK, LANES), 1)
    low = lane < HEAD_DIM
    ri = lax.broadcasted_iota(jnp.int32, (CHUNK, LANES), 0)
    ci = lane % HEAD_DIM
    eye2 = (ri == ci).astype(F32)
    ti = lax.broadcasted_iota(jnp.int32, (CHUNK, CHUNK), 0)
    tj = lax.broadcasted_iota(jnp.int32, (CHUNK, CHUNK), 1)
    neg_ones = jnp.full((CHUNK, CHUNK), -1.0, F32)

    dirs = []
    for tri, keep, upto, strict, last in (
            ((tj <= ti), ci <= ri, ci >= ri, ci < ri, CHUNK - 1),
            ((tj >= ti), ci >= ri, ci <= ri, ci > ri, 0)):
        trif = tri.astype(F32)
        lhs = jnp.concatenate([trif, neg_ones, trif, neg_ones], axis=1).astype(BF16)
        dirs.append((lhs, keep, upto, strict, last))

    def body(grp, states):
        states = list(states)
        done = [0]

        def recurrence_steps(count):
            for _ in range(count):
                k = grp * group + done[0]
                done[0] += 1
                for d, c in enumerate((k, n - 1 - k)):
                    rows = pl.ds(pl.multiple_of(c * CHUNK, CHUNK), CHUNK)
                    s2 = states[d]
                    lhs = jnp.concatenate([m_s[prev, d, c], p_s[prev, d, rows, :]], axis=0)
                    ms_ps = _dot(lhs, _block_diag(s2, low).astype(BF16))
                    out_s[d, rows, :] = ms_ps[CHUNK:] + r_s[prev, d, rows, :]
                    states[d] = s2 * gl_s[prev, d, c][0:1, :] + n_s[prev, d, c].astype(F32) - ms_ps[:CHUNK]

        rounds = 10
        schedule = iter([group // rounds + (1 if i < group % rounds else 0) for i in range(rounds)])
        between = lambda: recurrence_steps(next(schedule))

        r0 = pl.multiple_of(grp * grp_rows, grp_rows)
        rows_g = pl.ds(r0, grp_rows)
        kf = k_ref[rows_g, :].astype(F32)
        qf = q_ref[rows_g, :].astype(F32)
        vf = v_ref[rows_g, :].astype(F32)
        ghi, glo = _split2(gate_ref[rows_g, :])
        sp = _dot(jnp.concatenate([ghi, glo], axis=1), spread2)

        chunks = []
        for u in range(group):
            sl = slice(u * CHUNK, (u + 1) * CHUNK)
            kc, qc, vc = kf[sl], qf[sl], vf[sl]
            kq = jnp.concatenate([kc, qc], axis=0).astype(BF16)
            gq = _dot_nt(kq, _block_diag(kc, low).astype(BF16))
            chunks.append((kc, qc, vc, gq[:CHUNK], gq[CHUNK:], sp[sl]))
        between()

        dgs = []
        for kc, qc, vc, gram, qk, spc in chunks:
            for d, (lhs, keep, upto, strict, last) in enumerate(dirs):
                ghi2, glo2 = _split2(spc[:, (2 + d) * LANES:(3 + d) * LANES])
                zero = jnp.zeros_like(ghi2)
                rhs = jnp.concatenate([
                    jnp.concatenate([ghi2, jnp.where(upto, ghi2, zero), glo2, jnp.where(upto, glo2, zero)], axis=0),
                    jnp.concatenate([ghi2, zero, glo2, zero], axis=0)], axis=1)
                dgs.append(_dot(lhs, rhs))
        between()

        probs = []
        nmats = []
        it = iter(dgs)
        for kc, qc, vc, gram, qk, spc in chunks:
            for d, (lhs, keep, upto, strict, last) in enumerate(dirs):
                dg = next(it)
                beta = spc[:, d * LANES:(d + 1) * LANES]
                delta = dg[:, :LANES]
                gcol = dg[:, LANES:]
                decay = jnp.where(keep, jnp.exp(jnp.minimum(delta, 0.0)), 0.0)
                gamma = jnp.exp(gcol)
                glast = gcol[last:last + 1, :]
                nmats.append(jnp.where(strict, -(beta * gram * decay), 0.0))
                kd = (kc * jnp.exp(glast - gcol)).astype(BF16)
                probs.append((kc, qc, vc, beta, gamma, glast, qk * decay, kd))
        tinvs = _neumann(nmats, low, eye2, between)

        uws = []
        for (kc, qc, vc, beta, gamma, glast, qkd, kd), tinv in zip(probs, tinvs):
            rhs = jnp.concatenate([_block_diag(kc * (beta * gamma), low), _block_diag(vc * beta, low)], axis=1)
            uws.append(_mm(tinv, rhs))
        between()

        mns = []
        prs = []
        for (kc, qc, vc, beta, gamma, glast, qkd, kd), wu in zip(probs, uws):
            mns.append(_dot_tn(kd, wu.astype(BF16)))
            rhs = jnp.concatenate([_block_diag(wu[:, :LANES], low), _block_diag(wu[:, LANES:], low)], axis=1)
            prs.append(_mm(qkd, rhs))
        between()

        idx = 0
        for u in range(group):
            c = grp * group + u
            rows = pl.ds(pl.multiple_of(r0 + u * CHUNK, CHUNK), CHUNK)
            for d in range(2):
                kc, qc, vc, beta, gamma, glast, qkd, kd = probs[idx]
                mn, pr = mns[idx], prs[idx]
                m_s[cur, d, c] = _diag_blocks(mn[:, :LANES], low).astype(BF16)
                n_s[cur, d, c] = _diag_blocks(mn[:, LANES:], low).astype(BF16)
                p_s[cur, d, rows, :] = (qc * gamma - pr[:, :LANES]).astype(BF16)
                r_s[cur, d, rows, :] = pr[:, LANES:]
                gl_s[cur, d, c] = jnp.broadcast_to(jnp.exp(glast), (8, LANES))
                idx += 1
        assert done[0] == group
        return tuple(states)

    zero_state = jnp.zeros((CHUNK, LANES), F32)
    lax.fori_loop(0, n // group, body, (zero_state, zero_state))

    o = out_s[0] + out_s[1]
    ms = _seg_sum(o * o, ones_bd) * (1.0 / HEAD_DIM)
    o = o * lax.rsqrt(ms + RMS_EPS) * normo_ref[...]
    z = z_ref[...].astype(F32)
    o_ref[...] = (o * (z * _sigmoid(z))).astype(o_ref.dtype)


def _dn(qkv, z, gates, normo2, b, t):
    n = t // CHUNK
    n_items = b * N_PAIRS

    def cur_spec(base):
        def index(s):
            i = jnp.minimum(s, n_items - 1)
            return (i // N_PAIRS, 0, base + i % N_PAIRS)
        return pl.BlockSpec((None, t, LANES), index)

    def lag_index(s):
        i = jnp.maximum(s - 1, 0)
        return (i // N_PAIRS, 0, i % N_PAIRS)

    return pl.pallas_call(
        functools.partial(_dn_kernel, t=t, n_items=n_items),
        grid=(n_items + 1,),
        in_specs=[
            cur_spec(0), cur_spec(N_PAIRS), cur_spec(2 * N_PAIRS),
            pl.BlockSpec((None, t, LANES), lambda s: (jnp.minimum(s, n_items - 1) // N_PAIRS, 0, 0)),
            pl.BlockSpec((None, t, LANES), lag_index),
            _const_spec((1, LANES)),
        ],
        out_specs=pl.BlockSpec((None, t, LANES), lag_index),
        out_shape=jax.ShapeDtypeStruct((b, t, D_DN), BF16),
        scratch_shapes=[
            pltpu.VMEM((2, 2, n, CHUNK, LANES), BF16),
            pltpu.VMEM((2, 2, n, CHUNK, LANES), BF16),
            pltpu.VMEM((2, 2, t, LANES), BF16),
            pltpu.VMEM((2, 2, t, LANES), F32),
            pltpu.VMEM((2, 2, n, 8, LANES), F32),
            pltpu.VMEM((2, t, LANES), F32),
        ],
        compiler_params=_params(("arbitrary",)),
        name="dn",
    )(qkv, qkv, qkv, gates, z, normo2)


def _mem_kv_kernel(m_ref, g_ref, w_ref, k_ref, v_ref):
    mn = _rms(m_ref[...], g_ref[...]).astype(BF16)
    k_ref[...] = _dot(mn, w_ref[:, :D_MODEL]).astype(k_ref.dtype)
    v_ref[...] = _dot(mn, w_ref[:, D_MODEL:]).astype(v_ref.dtype)


def _mem_kv(mem, g, w_kv):
    b = mem.shape[0]
    blk = pl.BlockSpec((None, N_MEM, D_MODEL), lambda i: (i, 0, 0))
    return pl.pallas_call(
        _mem_kv_kernel,
        grid=(b,),
        in_specs=[blk, _const_spec((1, D_MODEL)), _const_spec((D_MODEL, 2 * D_MODEL))],
        out_specs=[blk, blk],
        out_shape=[jax.ShapeDtypeStruct((b, N_MEM, D_MODEL), BF16)] * 2,
        compiler_params=_params(("parallel",)),
        name="mem_kv",
    )(mem, g, w_kv)


def _mix_xattn_kernel(x_ref, yna_ref, ydn_ref, wout_ref, g_ref, wq_ref, k_ref, v_ref, wo_ref, o_ref):
    x = x_ref[...] + _dot(yna_ref[...], wout_ref[:D_NA, :]) + _dot(ydn_ref[...], wout_ref[D_NA:, :])
    xn = _rms(x, g_ref[...]).astype(BF16)
    q = (_dot(xn, wq_ref[...]) * (HEAD_DIM_X ** -0.5)).astype(BF16)
    head_cols = [slice(h * HEAD_DIM_X, (h + 1) * HEAD_DIM_X) for h in range(N_HEADS_X)]
    scores = [_dot_nt(q[:, cols], k_ref[:, cols]) for cols in head_cols]
    probs = []
    for s in scores:
        e = jnp.exp(s - jnp.max(s, axis=-1, keepdims=True))
        probs.append((e.astype(BF16), jnp.sum(e, axis=-1, keepdims=True)))
    heads = [(_dot(e, v_ref[:, cols]) * (1.0 / l)).astype(BF16) for (e, l), cols in zip(probs, head_cols)]
    o_ref[...] = x + _dot(jnp.concatenate(heads, axis=-1), wo_ref[...])


def _mix_xattn(x, y_na, y_dn, w_out, g, w_q, kmem, vmem, w_o):
    b, t, _ = x.shape
    tm = _token_block(t)
    tok = lambda d: pl.BlockSpec((None, tm, d), lambda i, j: (i, j, 0))
    memspec = pl.BlockSpec((None, N_MEM, D_MODEL), lambda i, j: (i, 0, 0))
    return pl.pallas_call(
        _mix_xattn_kernel,
        grid=(b, t // tm),
        in_specs=[
            tok(D_MODEL), tok(D_NA), tok(D_DN),
            _const_spec((D_NA + D_DN, D_MODEL)),
            _const_spec((1, D_MODEL)),
            _const_spec((D_MODEL, D_MODEL)),
            memspec, memspec,
            _const_spec((D_MODEL, D_MODEL)),
        ],
        out_specs=tok(D_MODEL),
        out_shape=jax.ShapeDtypeStruct((b, t, D_MODEL), F32),
        compiler_params=_params(("parallel", "parallel")),
        name="mix_xattn",
    )(x, y_na, y_dn, w_out, g, w_q, kmem, vmem, w_o)


def _ffn_kernel(x_ref, xp_ref, xnx_ref, g_ref, wv_ref, wg_ref, cv_ref, cg_ref, bv_ref, bg_ref, wd_ref, gf_ref, o_ref,
                *, tm, final):
    xe = _normed_with_halo(x_ref, xp_ref, xnx_ref, g_ref[...])
    ups = [(_dot(xe, wv_ref[c]), _dot(xe, wg_ref[c])) for c in range(FFN_AHEAD)]
    acts = []
    for c in range(N_FF_CHUNKS):
        hv, hg = ups.pop(0)
        if c + FFN_AHEAD < N_FF_CHUNKS:
            ups.append((_dot(xe, wv_ref[c + FFN_AHEAD]), _dot(xe, wg_ref[c + FFN_AHEAD])))
        val = _token_conv3(hv, cv_ref[c], tm) + bv_ref[c]
        gate = _token_conv3(hg, cg_ref[c], tm) + bg_ref[c]
        acts.append((gate * _sigmoid(gate) * val).astype(BF16))
    acc = x_ref[...] + _dot(jnp.concatenate(acts, axis=1), wd_ref[...])
    if final:
        acc = _rms(acc, gf_ref[...])
    o_ref[...] = acc


def _ffn(x, g, wv, wg, cv, cg, bv, bg, wd, g_final, final):
    b, t, _ = x.shape
    tm = _token_block(t)
    return pl.pallas_call(
        functools.partial(_ffn_kernel, tm=tm, final=final),
        grid=(b, t // tm),
        in_specs=_halo_specs(tm, t, D_MODEL) + [
            _const_spec((1, D_MODEL)),
            _const_spec((N_FF_CHUNKS, D_MODEL, FF_CHUNK)),
            _const_spec((N_FF_CHUNKS, D_MODEL, FF_CHUNK)),
            _const_spec((N_FF_CHUNKS, 3, FF_CHUNK)),
            _const_spec((N_FF_CHUNKS, 3, FF_CHUNK)),
            _const_spec((N_FF_CHUNKS, 1, FF_CHUNK)),
            _const_spec((N_FF_CHUNKS, 1, FF_CHUNK)),
            _const_spec((D_FF, D_MODEL)),
            _const_spec((1, D_MODEL)),
        ],
        out_specs=pl.BlockSpec((None, tm, D_MODEL), lambda i, j: (i, j, 0)),
        out_shape=jax.ShapeDtypeStruct((b, t, D_MODEL), F32),
        compiler_params=_params(("parallel", "parallel")),
        name="ffn",
    )(x, x, x, g, wv, wg, cv, cg, bv, bg, wd, g_final)


def _chunk_cols(w):
    rows = w.shape[0]
    parts = w.reshape(rows, 2, N_FF_CHUNKS, FF_CHUNK)
    return jnp.transpose(parts[:, 0], (1, 0, 2)), jnp.transpose(parts[:, 1], (1, 0, 2))


def _prep_layer(l, norm_mix, w_in, rpb, conv_qkv, a_log, dt_bias, norm_o, w_out, norm_x, norm_mem, w_xq, w_xkv, w_xo,
                norm_ffn, w_up, conv_ffn, conv_ffn_b, w_down):
    row = lambda v: v.reshape(1, -1).astype(F32)
    w_gate = jnp.pad(w_in[l][:, D_MAIN:], ((0, 0), (0, LANES - N_GATE))).astype(BF16)
    pad = jnp.zeros((2 * N_HEADS_DN,), F32)
    tail = jnp.zeros((LANES - N_GATE,), F32)
    prm = jnp.stack([jnp.concatenate([pad, a_log[l].reshape(-1), tail]),
                     jnp.concatenate([pad, dt_bias[l].reshape(-1), tail])])
    wv, wg = _chunk_cols(w_up[l].astype(BF16))
    cv, cg = _chunk_cols(conv_ffn[l])
    bv, bg = _chunk_cols(conv_ffn_b[l].reshape(1, -1))
    return dict(
        norm_mix=row(norm_mix[l]), w_main=w_in[l][:, :D_MAIN].astype(BF16), w_gate=w_gate,
        na_bias=_na_bias_tables(rpb[l]), conv_qkv=conv_qkv[l], prm=prm,
        norm_o=row(jnp.concatenate([norm_o[l], norm_o[l]])), w_out=w_out[l].astype(BF16),
        norm_x=row(norm_x[l]), norm_mem=row(norm_mem[l]), w_xq=w_xq[l].astype(BF16), w_xkv=w_xkv[l].astype(BF16),
        w_xo=w_xo[l].astype(BF16), norm_ffn=row(norm_ffn[l]), wv=wv, wg=wg, cv=cv, cg=cg, bv=bv, bg=bg,
        wd=w_down[l].astype(BF16))


def _layer(x, mem, p, g_final, final):
    b, t, _ = x.shape
    qkv_na, qkv_dn, z, gates = _in_proj(x, p["norm_mix"], p["w_main"], p["w_gate"], p["conv_qkv"], p["prm"])
    y_na = _na(qkv_na, p["na_bias"], b, t)
    y_dn = _dn(qkv_dn, z, gates, p["norm_o"], b, t)
    kmem, vmem = _mem_kv(mem, p["norm_mem"], p["w_xkv"])
    x = _mix_xattn(x, y_na, y_dn, p["w_out"], p["norm_x"], p["w_xq"], kmem, vmem, p["w_xo"])
    return _ffn(x, p["norm_ffn"], p["wv"], p["wg"], p["cv"], p["cg"], p["bv"], p["bg"], p["wd"], g_final, final)


def kernel(x_prompt, x_sample, mem_prompt, mem_sample, norm_mix, w_in, rpb, conv_qkv, a_log, dt_bias, norm_o, w_out,
           norm_x, norm_mem, w_xq, w_xkv, w_xo, norm_ffn, w_up, conv_ffn, conv_ffn_b, w_down, norm_final):
    layers = [_prep_layer(l, norm_mix, w_in, rpb, conv_qkv, a_log, dt_bias, norm_o, w_out, norm_x, norm_mem, w_xq,
                          w_xkv, w_xo, norm_ffn, w_up, conv_ffn, conv_ffn_b, w_down) for l in range(DEPTH)]
    g_final = norm_final.reshape(1, -1).astype(F32)
    outs = []
    for x, mem in ((x_prompt, mem_prompt), (x_sample, mem_sample)):
        for l in range(DEPTH):
            x = _layer(x, mem, layers[l], g_final, l == DEPTH - 1)
        outs.append(x)
    return tuple(outs)
```

```python
import functools

import numpy as np
import jax
import jax.numpy as jnp
from jax import lax
from jax.experimental import pallas as pl
from jax.experimental.pallas import tpu as pltpu

F32 = jnp.float32
BF16 = jnp.bfloat16

D_MODEL = 1024
DEPTH = 4
HEAD_DIM = 64
N_HEADS_NA = 8
N_HEADS_DN = 8
D_NA = N_HEADS_NA * HEAD_DIM
D_DN = N_HEADS_DN * HEAD_DIM
GRID_W = 64
WIN_H = 8
WIN_W = 16
CHUNK = 64
N_MEM = 256
N_HEADS_X = 4
HEAD_DIM_X = D_MODEL // N_HEADS_X
D_FF = 2816
RMS_EPS = 1e-6
L2_EPS = 1e-6

LANES = 128
MXU_DIM = 256
N_PAIRS = N_HEADS_DN // 2
D_MAIN = 3 * D_NA + 4 * D_DN
N_GATE = 4 * N_HEADS_DN
FF_CHUNK = 256
N_FF_CHUNKS = D_FF // FF_CHUNK
FFN_AHEAD = 2
HALO = 8
NA_ROWS = 8
NA_AHEAD = 4
DN_GROUP = 8
VMEM_LIMIT = 56 * 1024 * 1024


def _token_block(t):
    return min(1024, t)


def _rms(x, g):
    return x * lax.rsqrt(jnp.mean(x * x, axis=-1, keepdims=True) + RMS_EPS) * g


def _sigmoid(x):
    return 1.0 / (1.0 + jnp.exp(-x))


def _split2(x):
    hi = x.astype(BF16)
    lo = (x - hi.astype(F32)).astype(BF16)
    return hi, lo


def _dot(a, b):
    return jnp.dot(a, b, preferred_element_type=F32)


def _dot_nt(a, b):
    return lax.dot_general(a, b, (((1,), (1,)), ((), ())), preferred_element_type=F32)


def _dot_tn(a, b):
    return lax.dot_general(a, b, (((0,), (0,)), ((), ())), preferred_element_type=F32)


def _head_block_ones(n):
    r = lax.broadcasted_iota(jnp.int32, (n, n), 0)
    c = lax.broadcasted_iota(jnp.int32, (n, n), 1)
    return ((r // HEAD_DIM) == (c // HEAD_DIM)).astype(BF16)


def _const_spec(shape):
    nd = len(shape)
    return pl.BlockSpec(shape, lambda *_: (0,) * nd)


def _params(sem):
    return pltpu.CompilerParams(dimension_semantics=sem, vmem_limit_bytes=VMEM_LIMIT)


def _halo_specs(tm, t, d):
    nh = tm // HALO
    last = t // HALO - 1
    return [
        pl.BlockSpec((None, tm, d), lambda i, j: (i, j, 0)),
        pl.BlockSpec((None, HALO, d), lambda i, j: (i, jnp.maximum(j * nh - 1, 0), 0)),
        pl.BlockSpec((None, HALO, d), lambda i, j: (i, jnp.minimum((j + 1) * nh, last), 0)),
    ]


def _normed_with_halo(x_ref, xp_ref, xnx_ref, g):
    j = pl.program_id(1)
    has_prev = jnp.where(j > 0, 1.0, 0.0)
    has_next = jnp.where(j < pl.num_programs(1) - 1, 1.0, 0.0)
    return jnp.concatenate([_rms(xp_ref[...], g) * has_prev, _rms(x_ref[...], g), _rms(xnx_ref[...], g) * has_next],
                           axis=0).astype(BF16)


def _token_conv3(h, w, tm):
    ext = tm + 2 * HALO
    hp = pltpu.roll(h, 1, 0)[HALO:HALO + tm]
    hn = pltpu.roll(h, ext - 1, 0)[HALO:HALO + tm]
    return hp * w[0:1] + h[HALO:HALO + tm] * w[1:2] + hn * w[2:3]


def _in_proj_kernel(x_ref, xp_ref, xnx_ref, g_ref, wm_ref, wg_ref, cw_ref, prm_ref, na_ref, dn_ref, z_ref, gate_ref,
                    *, tm):
    xe = _normed_with_halo(x_ref, xp_ref, xnx_ref, g_ref[...])
    xm = xe[HALO:HALO + tm]

    def na_part(j):
        r = _dot(xm, wm_ref[:, j * D_NA:(j + 1) * D_NA])
        if j == 0:
            r = r * (HEAD_DIM ** -0.5)
        na_ref[:, j * D_NA:(j + 1) * D_NA] = r.astype(na_ref.dtype)

    def z_part():
        off = 3 * D_NA + 3 * D_DN
        z_ref[...] = _dot(xm, wm_ref[:, off:off + D_DN]).astype(z_ref.dtype)

    def gate_part():
        raw = _dot(xm, wg_ref[...])
        col = lax.broadcasted_iota(jnp.int32, raw.shape, 1)
        xx = raw + prm_ref[1:2]
        softplus = jnp.maximum(xx, 0.0) + jnp.log(1.0 + jnp.exp(-jnp.abs(xx)))
        gate_ref[...] = jnp.where(col < 2 * N_HEADS_DN, _sigmoid(raw), -jnp.exp(prm_ref[0:1]) * softplus)

    fillers = [functools.partial(na_part, 0), functools.partial(na_part, 1), functools.partial(na_part, 2),
               z_part, gate_part]

    ones_bd = _head_block_ones(MXU_DIM)
    n_blocks = 3 * D_DN // MXU_DIM

    def dn_proj(c):
        off = 3 * D_NA + c * MXU_DIM
        return _dot(xe, wm_ref[:, off:off + MXU_DIM])

    h_next = dn_proj(0)
    for c in range(n_blocks):
        h = h_next
        if c + 1 < n_blocks:
            h_next = dn_proj(c + 1)
        if fillers:
            fillers.pop(0)()
        cols = slice(c * MXU_DIM, (c + 1) * MXU_DIM)
        y = _token_conv3(h, cw_ref[:, cols], tm)
        y = y * _sigmoid(y)
        if c < 2 * D_DN // MXU_DIM:
            y = y * lax.rsqrt(_dot((y * y).astype(BF16), ones_bd) + L2_EPS)
        if c < D_DN // MXU_DIM:
            y = y * (HEAD_DIM ** -0.5)
        dn_ref[:, cols] = y.astype(dn_ref.dtype)
    for f in fillers:
        f()


def _in_proj(x, g, w_main, w_gate, conv_w, prm):
    b, t, _ = x.shape
    tm = _token_block(t)
    tok = lambda d: pl.BlockSpec((None, tm, d), lambda i, j: (i, j, 0))
    return pl.pallas_call(
        functools.partial(_in_proj_kernel, tm=tm),
        grid=(b, t // tm),
        in_specs=_halo_specs(tm, t, D_MODEL) + [
            _const_spec((1, D_MODEL)),
            _const_spec((D_MODEL, D_MAIN)),
            _const_spec((D_MODEL, LANES)),
            _const_spec((3, 3 * D_DN)),
            _const_spec((2, LANES)),
        ],
        out_specs=[tok(3 * D_NA), tok(3 * D_DN), tok(D_DN), tok(LANES)],
        out_shape=[
            jax.ShapeDtypeStruct((b, t, 3 * D_NA), BF16),
            jax.ShapeDtypeStruct((b, t, 3 * D_DN), BF16),
            jax.ShapeDtypeStruct((b, t, D_DN), BF16),
            jax.ShapeDtypeStruct((b, t, LANES), F32),
        ],
        compiler_params=_params(("parallel", "parallel")),
        name="in_proj",
    )(x, x, x, g, w_main, w_gate, conv_w, prm)


def _na_bias_tables(rpb_l):
    c = np.arange(GRID_W)
    cs = np.clip(c - WIN_W // 2, 0, GRID_W - WIN_W)
    valid = (c[None, :] >= cs[:, None]) & (c[None, :] < cs[:, None] + WIN_W)
    coff = np.clip(c[None, :] - c[:, None] + (WIN_W - 1), 0, 2 * WIN_W - 2)
    roff = np.arange(WIN_H)[None, :] + (WIN_H - 1) - np.arange(WIN_H)[:, None]
    select = np.asarray(coff[..., None] == np.arange(2 * WIN_W - 1), np.float32)
    b = jnp.einsum("vhib,qkb->vhqik", jnp.transpose(rpb_l[:, roff], (1, 0, 2, 3)), select,
                   precision=lax.Precision.HIGHEST)
    b = jnp.where(valid[None, None, :, None, :], b, -jnp.inf)
    return b.reshape(WIN_H, N_PAIRS, 2 * GRID_W, WIN_H * GRID_W).astype(F32)


def _na_kernel(q_ref, k_ref, v_ref, bias_ref, o_ref, *, rows):
    step = pl.program_id(1)
    nk = WIN_H * GRID_W
    lane = lax.broadcasted_iota(jnp.int32, (GRID_W, LANES), 1)
    low = lane < HEAD_DIM

    problems = []
    for rr in range(NA_ROWS):
        r = step * NA_ROWS + rr
        rs = jnp.clip(r - WIN_H // 2, 0, rows - WIN_H)
        start = pl.multiple_of(rs * GRID_W, GRID_W)
        for p in range(N_PAIRS):
            problems.append((rr, p, r - rs, start))

    def scores(rr, p, var, start):
        cols = slice(p * LANES, (p + 1) * LANES)
        q2 = q_ref[rr * GRID_W:(rr + 1) * GRID_W, cols].astype(F32)
        qs = jnp.concatenate([jnp.where(low, q2, 0.0), jnp.where(low, 0.0, q2)], axis=0).astype(BF16)
        return _dot_nt(qs, k_ref[pl.ds(start, nk), cols]) + bias_ref[var, p]

    pending = [scores(*prob) for prob in problems[:NA_AHEAD]]
    for i, (rr, p, var, start) in enumerate(problems):
        s = pending.pop(0)
        if i + NA_AHEAD < len(problems):
            pending.append(scores(*problems[i + NA_AHEAD]))
        e = jnp.exp(s - jnp.max(s, axis=-1, keepdims=True))
        l = jnp.sum(e, axis=-1, keepdims=True)
        cols = slice(p * LANES, (p + 1) * LANES)
        o = _dot(e.astype(BF16), v_ref[pl.ds(start, nk), cols]) * (1.0 / l)
        o_ref[rr * GRID_W:(rr + 1) * GRID_W, cols] = jnp.where(low, o[:GRID_W], o[GRID_W:]).astype(o_ref.dtype)


def _na(qkv, bias, b, t):
    rows = t // GRID_W
    blk = NA_ROWS * GRID_W
    return pl.pallas_call(
        functools.partial(_na_kernel, rows=rows),
        grid=(b, rows // NA_ROWS),
        in_specs=[
            pl.BlockSpec((None, blk, D_NA), lambda i, r: (i, r, 0)),
            pl.BlockSpec((None, t, D_NA), lambda i, r: (i, 0, 1)),
            pl.BlockSpec((None, t, D_NA), lambda i, r: (i, 0, 2)),
            _const_spec((WIN_H, N_PAIRS, 2 * GRID_W, WIN_H * GRID_W)),
        ],
        out_specs=pl.BlockSpec((None, blk, D_NA), lambda i, r: (i, r, 0)),
        out_shape=jax.ShapeDtypeStruct((b, t, D_NA), BF16),
        compiler_params=_params(("parallel", "arbitrary")),
        name="na",
    )(qkv, qkv, qkv, bias)


def _block_diag(x, low):
    return jnp.concatenate([jnp.where(low, x, 0.0), jnp.where(low, 0.0, x)], axis=0)


def _diag_blocks(x, low):
    return jnp.where(low, x[:HEAD_DIM], x[HEAD_DIM:])


def _mm(a, b):
    return _dot(a.astype(BF16), b.astype(BF16))


def _neumann(nmats, low, eye2, between):
    ps = [_mm(nm, _block_diag(nm, low)) for nm in nmats]
    ts = [eye2 + nm for nm in nmats]
    between()
    for _ in range(4):
        outs = [_mm(p, jnp.concatenate([_block_diag(p, low), _block_diag(tt, low)], axis=1)) for p, tt in zip(ps, ts)]
        ps = [o[:, :LANES] for o in outs]
        ts = [tt + o[:, LANES:] for tt, o in zip(ts, outs)]
        between()
    res = [tt + _mm(p, _block_diag(tt, low)) for p, tt in zip(ps, ts)]
    between()
    return res


def _dn_kernel(q_ref, k_ref, v_ref, gate_ref, z_ref, normo_ref, o_ref, m_s, n_s, p_s, r_s, gl_s, out_s,
               *, t, n_items):
    n = t // CHUNK
    group = min(DN_GROUP, n)
    grp_rows = group * CHUNK
    step = pl.program_id(0)
    cur = step % 2
    prev = 1 - cur
    pair = jnp.minimum(step, n_items - 1) % N_PAIRS

    @pl.when(step == 0)
    def _():
        m_s[1] = jnp.zeros(m_s.shape[1:], m_s.dtype)
        n_s[1] = jnp.zeros(n_s.shape[1:], n_s.dtype)
        p_s[1] = jnp.zeros(p_s.shape[1:], p_s.dtype)
        r_s[1] = jnp.zeros(r_s.shape[1:], r_s.dtype)
        gl_s[1] = jnp.zeros(gl_s.shape[1:], gl_s.dtype)

    er = lax.broadcasted_iota(jnp.int32, (LANES, 4 * LANES), 0)
    ec = lax.broadcasted_iota(jnp.int32, (LANES, 4 * LANES), 1)
    src = (ec // LANES) * N_HEADS_DN + 2 * pair + (ec % LANES) // HEAD_DIM
    spread = (er == src).astype(BF16)
    spread2 = jnp.concatenate([spread, spread], axis=0)

    lane = lax.broadcasted_iota(jnp.int32, (CHUNK, LANES), 1)
    low = lane < HEAD_DIM
    ri = lax.broadcasted_iota(jnp.int32, (CHUNK, LANES), 0)
    ci = lane % HEAD_DIM
    eye2 = (ri == ci).astype(F32)
    ti = lax.broadcasted_iota(jnp.int32, (CHUNK, CHUNK), 0)
    tj = lax.broadcasted_iota(jnp.int32, (CHUNK, CHUNK), 1)
    neg_ones = jnp.full((CHUNK, CHUNK), -1.0, F32)

    dirs = []
    for tri, keep, upto, strict, last in (
            ((tj <= ti), ci <= ri, ci >= ri, ci < ri, CHUNK - 1),
            ((tj >= ti), ci >= ri, ci <= ri, ci > ri, 0)):
        trif = tri.astype(F32)
        lhs = jnp.concatenate([trif, neg_ones, trif, neg_ones], axis=1).astype(BF16)
        dirs.append((lhs, keep, upto, strict, last))

    def body(grp, states):
        states = list(states)
        done = [0]

        def recurrence_steps(count):
            for _ in range(count):
                k = grp * group + done[0]
                done[0] += 1
                for d, c in enumerate((k, n - 1 - k)):
                    rows = pl.ds(pl.multiple_of(c * CHUNK, CHUNK), CHUNK)
                    s2 = states[d]
                    lhs = jnp.concatenate([m_s[prev, d, c], p_s[prev, d, rows, :]], axis=0)
                    ms_ps = _dot(lhs, _block_diag(s2, low).astype(BF16))
                    out_s[d, rows, :] = ms_ps[CHUNK:] + r_s[prev, d, rows, :]
                    states[d] = s2 * gl_s[prev, d, c][0:1, :] + n_s[prev, d, c].astype(F32) - ms_ps[:CHUNK]

        rounds = 10
        schedule = iter([group // rounds + (1 if i < group % rounds else 0) for i in range(rounds)])
        between = lambda: recurrence_steps(next(schedule))

        r0 = pl.multiple_of(grp * grp_rows, grp_rows)
        rows_g = pl.ds(r0, grp_rows)
        kf = k_ref[rows_g, :].astype(F32)
        qf = q_ref[rows_g, :].astype(F32)
        vf = v_ref[rows_g, :].astype(F32)
        ghi, glo = _split2(gate_ref[rows_g, :])
        sp = _dot(jnp.concatenate([ghi, glo], axis=1), spread2)

        chunks = []
        for u in range(group):
            sl = slice(u * CHUNK, (u + 1) * CHUNK)
            kc, qc, vc = kf[sl], qf[sl], vf[sl]
            kq = jnp.concatenate([kc, qc], axis=0).astype(BF16)
            gq = _dot_nt(kq, _block_diag(kc, low).astype(BF16))
            chunks.append((kc, qc, vc, gq[:CHUNK], gq[CHUNK:], sp[sl]))
        between()

        dgs = []
        for kc, qc, vc, gram, qk, spc in chunks:
            for d, (lhs, keep, upto, strict, last) in enumerate(dirs):
                ghi2, glo2 = _split2(spc[:, (2 + d) * LANES:(3 + d) * LANES])
                zero = jnp.zeros_like(ghi2)
                rhs = jnp.concatenate([
                    jnp.concatenate([ghi2, jnp.where(upto, ghi2, zero), glo2, jnp.where(upto, glo2, zero)], axis=0),
                    jnp.concatenate([ghi2, zero, glo2, zero], axis=0)], axis=1)
                dgs.append(_dot(lhs, rhs))
        between()

        probs = []
        nmats = []
        it = iter(dgs)
        for kc, qc, vc, gram, qk, spc in chunks:
            for d, (lhs, keep, upto, strict, last) in enumerate(dirs):
                dg = next(it)
                beta = spc[:, d * LANES:(d + 1) * LANES]
                delta = dg[:, :LANES]
                gcol = dg[:, LANES:]
                decay = jnp.where(keep, jnp.exp(jnp.minimum(delta, 0.0)), 0.0)
                gamma = jnp.exp(gcol)
                glast = gcol[last:last + 1, :]
                nmats.append(jnp.where(strict, -(beta * gram * decay), 0.0))
                kd = (kc * jnp.exp(glast - gcol)).astype(BF16)
                probs.append((kc, qc, vc, beta, gamma, glast, qk * decay, kd))
        tinvs = _neumann(nmats, low, eye2, between)

        uws = []
        for (kc, qc, vc, beta, gamma, glast, qkd, kd), tinv in zip(probs, tinvs):
            rhs = jnp.concatenate([_block_diag(kc * (beta * gamma), low), _block_diag(vc * beta, low)], axis=1)
            uws.append(_mm(tinv, rhs))
        between()

        mns = []
        prs = []
        for (kc, qc, vc, beta, gamma, glast, qkd, kd), wu in zip(probs, uws):
            mns.append(_dot_tn(kd, wu.astype(BF16)))
            rhs = jnp.concatenate([_block_diag(wu[:, :LANES], low), _block_diag(wu[:, LANES:], low)], axis=1)
            prs.append(_mm(qkd, rhs))
        between()

        idx = 0
        for u in range(group):
            c = grp * group + u
            rows = pl.ds(pl.multiple_of(r0 + u * CHUNK, CHUNK), CHUNK)
            for d in range(2):
                kc, qc, vc, beta, gamma, glast, qkd, kd = probs[idx]
                mn, pr = mns[idx], prs[idx]
                m_s[cur, d, c] = _diag_blocks(mn[:, :LANES], low).astype(BF16)
                n_s[cur, d, c] = _diag_blocks(mn[:, LANES:], low).astype(BF16)
                p_s[cur, d, rows, :] = (qc * gamma - pr[:, :LANES]).astype(BF16)
                r_s[cur, d, rows, :] = pr[:, LANES:]
                gl_s[cur, d, c] = jnp.broadcast_to(jnp.exp(glast), (8, LANES))
                idx += 1
        assert done[0] == group
        return tuple(states)

    zero_state = jnp.zeros((CHUNK, LANES), F32)
    lax.fori_loop(0, n // group, body, (zero_state, zero_state))

    o = out_s[0] + out_s[1]
    sq = (o * o).astype(BF16)
    ss = _dot(jnp.concatenate([sq[:t // 2], sq[t // 2:]], axis=1), _head_block_ones(MXU_DIM))
    ms = jnp.concatenate([ss[:, :LANES], ss[:, LANES:]], axis=0) * (1.0 / HEAD_DIM)
    o = o * lax.rsqrt(ms + RMS_EPS) * normo_ref[...]
    z = z_ref[...].astype(F32)
    o_ref[...] = (o * (z * _sigmoid(z))).astype(o_ref.dtype)


def _dn(qkv, z, gates, normo2, b, t):
    n = t // CHUNK
    n_items = b * N_PAIRS

    def cur_spec(base):
        def index(s):
            i = jnp.minimum(s, n_items - 1)
            return (i // N_PAIRS, 0, base + i % N_PAIRS)
        return pl.BlockSpec((None, t, LANES), index)

    def lag_index(s):
        i = jnp.maximum(s - 1, 0)
        return (i // N_PAIRS, 0, i % N_PAIRS)

    return pl.pallas_call(
        functools.partial(_dn_kernel, t=t, n_items=n_items),
        grid=(n_items + 1,),
        in_specs=[
            cur_spec(0), cur_spec(N_PAIRS), cur_spec(2 * N_PAIRS),
            pl.BlockSpec((None, t, LANES), lambda s: (jnp.minimum(s, n_items - 1) // N_PAIRS, 0, 0)),
            pl.BlockSpec((None, t, LANES), lag_index),
            _const_spec((1, LANES)),
        ],
        out_specs=pl.BlockSpec((None, t, LANES), lag_index),
        out_shape=jax.ShapeDtypeStruct((b, t, D_DN), BF16),
        scratch_shapes=[
            pltpu.VMEM((2, 2, n, CHUNK, LANES), BF16),
            pltpu.VMEM((2, 2, n, CHUNK, LANES), BF16),
            pltpu.VMEM((2, 2, t, LANES), BF16),
            pltpu.VMEM((2, 2, t, LANES), F32),
            pltpu.VMEM((2, 2, n, 8, LANES), F32),
            pltpu.VMEM((2, t, LANES), F32),
        ],
        compiler_params=_params(("arbitrary",)),
        name="dn",
    )(qkv, qkv, qkv, gates, z, normo2)


def _mem_kv_kernel(m_ref, g_ref, w_ref, k_ref, v_ref):
    mn = _rms(m_ref[...], g_ref[...]).astype(BF16)
    k_ref[...] = _dot(mn, w_ref[:, :D_MODEL]).astype(k_ref.dtype)
    v_ref[...] = _dot(mn, w_ref[:, D_MODEL:]).astype(v_ref.dtype)


def _mem_kv(mem, g, w_kv):
    b = mem.shape[0]
    blk = pl.BlockSpec((None, N_MEM, D_MODEL), lambda i: (i, 0, 0))
    return pl.pallas_call(
        _mem_kv_kernel,
        grid=(b,),
        in_specs=[blk, _const_spec((1, D_MODEL)), _const_spec((D_MODEL, 2 * D_MODEL))],
        out_specs=[blk, blk],
        out_shape=[jax.ShapeDtypeStruct((b, N_MEM, D_MODEL), BF16)] * 2,
        compiler_params=_params(("parallel",)),
        name="mem_kv",
    )(mem, g, w_kv)


def _mix_xattn_kernel(x_ref, yna_ref, ydn_ref, wout_ref, g_ref, wq_ref, k_ref, v_ref, wo_ref, o_ref):
    x = x_ref[...] + _dot(yna_ref[...], wout_ref[:D_NA, :]) + _dot(ydn_ref[...], wout_ref[D_NA:, :])
    xn = _rms(x, g_ref[...]).astype(BF16)
    q = (_dot(xn, wq_ref[...]) * (HEAD_DIM_X ** -0.5)).astype(BF16)
    head_cols = [slice(h * HEAD_DIM_X, (h + 1) * HEAD_DIM_X) for h in range(N_HEADS_X)]
    scores = [_dot_nt(q[:, cols], k_ref[:, cols]) for cols in head_cols]
    probs = []
    for s in scores:
        e = jnp.exp(s - jnp.max(s, axis=-1, keepdims=True))
        probs.append((e.astype(BF16), jnp.sum(e, axis=-1, keepdims=True)))
    heads = [(_dot(e, v_ref[:, cols]) * (1.0 / l)).astype(BF16) for (e, l), cols in zip(probs, head_cols)]
    o_ref[...] = x + _dot(jnp.concatenate(heads, axis=-1), wo_ref[...])


def _mix_xattn(x, y_na, y_dn, w_out, g, w_q, kmem, vmem, w_o):
    b, t, _ = x.shape
    tm = _token_block(t)
    tok = lambda d: pl.BlockSpec((None, tm, d), lambda i, j: (i, j, 0))
    memspec = pl.BlockSpec((None, N_MEM, D_MODEL), lambda i, j: (i, 0, 0))
    return pl.pallas_call(
        _mix_xattn_kernel,
        grid=(b, t // tm),
        in_specs=[
            tok(D_MODEL), tok(D_NA), tok(D_DN),
            _const_spec((D_NA + D_DN, D_MODEL)),
            _const_spec((1, D_MODEL)),
            _const_spec((D_MODEL, D_MODEL)),
            memspec, memspec,
            _const_spec((D_MODEL, D_MODEL)),
        ],
        out_specs=tok(D_MODEL),
        out_shape=jax.ShapeDtypeStruct((b, t, D_MODEL), F32),
        compiler_params=_params(("parallel", "parallel")),
        name="mix_xattn",
    )(x, y_na, y_dn, w_out, g, w_q, kmem, vmem, w_o)


def _ffn_kernel(x_ref, xp_ref, xnx_ref, g_ref, wv_ref, wg_ref, cv_ref, cg_ref, bv_ref, bg_ref, wd_ref, gf_ref, o_ref,
                *, tm, final):
    xe = _normed_with_halo(x_ref, xp_ref, xnx_ref, g_ref[...])
    ups = [(_dot(xe, wv_ref[c]), _dot(xe, wg_ref[c])) for c in range(FFN_AHEAD)]
    acts = []
    for c in range(N_FF_CHUNKS):
        hv, hg = ups.pop(0)
        if c + FFN_AHEAD < N_FF_CHUNKS:
            ups.append((_dot(xe, wv_ref[c + FFN_AHEAD]), _dot(xe, wg_ref[c + FFN_AHEAD])))
        val = _token_conv3(hv, cv_ref[c], tm) + bv_ref[c]
        gate = _token_conv3(hg, cg_ref[c], tm) + bg_ref[c]
        acts.append((gate * _sigmoid(gate) * val).astype(BF16))
    acc = x_ref[...] + _dot(jnp.concatenate(acts, axis=1), wd_ref[...])
    if final:
        acc = _rms(acc, gf_ref[...])
    o_ref[...] = acc


def _ffn(x, g, wv, wg, cv, cg, bv, bg, wd, g_final, final):
    b, t, _ = x.shape
    tm = _token_block(t)
    return pl.pallas_call(
        functools.partial(_ffn_kernel, tm=tm, final=final),
        grid=(b, t // tm),
        in_specs=_halo_specs(tm, t, D_MODEL) + [
            _const_spec((1, D_MODEL)),
            _const_spec((N_FF_CHUNKS, D_MODEL, FF_CHUNK)),
            _const_spec((N_FF_CHUNKS, D_MODEL, FF_CHUNK)),
            _const_spec((N_FF_CHUNKS, 3, FF_CHUNK)),
            _const_spec((N_FF_CHUNKS, 3, FF_CHUNK)),
            _const_spec((N_FF_CHUNKS, 1, FF_CHUNK)),
            _const_spec((N_FF_CHUNKS, 1, FF_CHUNK)),
            _const_spec((D_FF, D_MODEL)),
            _const_spec((1, D_MODEL)),
        ],
        out_specs=pl.BlockSpec((None, tm, D_MODEL), lambda i, j: (i, j, 0)),
        out_shape=jax.ShapeDtypeStruct((b, t, D_MODEL), F32),
        compiler_params=_params(("parallel", "parallel")),
        name="ffn",
    )(x, x, x, g, wv, wg, cv, cg, bv, bg, wd, g_final)


def _chunk_cols(w):
    rows = w.shape[0]
    parts = w.reshape(rows, 2, N_FF_CHUNKS, FF_CHUNK)
    return jnp.transpose(parts[:, 0], (1, 0, 2)), jnp.transpose(parts[:, 1], (1, 0, 2))


def _prep_layer(l, norm_mix, w_in, rpb, conv_qkv, a_log, dt_bias, norm_o, w_out, norm_x, norm_mem, w_xq, w_xkv, w_xo,
                norm_ffn, w_up, conv_ffn, conv_ffn_b, w_down):
    row = lambda v: v.reshape(1, -1).astype(F32)
    w_gate = jnp.pad(w_in[l][:, D_MAIN:], ((0, 0), (0, LANES - N_GATE))).astype(BF16)
    pad = jnp.zeros((2 * N_HEADS_DN,), F32)
    tail = jnp.zeros((LANES - N_GATE,), F32)
    prm = jnp.stack([jnp.concatenate([pad, a_log[l].reshape(-1), tail]),
                     jnp.concatenate([pad, dt_bias[l].reshape(-1), tail])])
    wv, wg = _chunk_cols(w_up[l].astype(BF16))
    cv, cg = _chunk_cols(conv_ffn[l])
    bv, bg = _chunk_cols(conv_ffn_b[l].reshape(1, -1))
    return dict(
        norm_mix=row(norm_mix[l]), w_main=w_in[l][:, :D_MAIN].astype(BF16), w_gate=w_gate,
        na_bias=_na_bias_tables(rpb[l]), conv_qkv=conv_qkv[l], prm=prm,
        norm_o=row(jnp.concatenate([norm_o[l], norm_o[l]])), w_out=w_out[l].astype(BF16),
        norm_x=row(norm_x[l]), norm_mem=row(norm_mem[l]), w_xq=w_xq[l].astype(BF16), w_xkv=w_xkv[l].astype(BF16),
        w_xo=w_xo[l].astype(BF16), norm_ffn=row(norm_ffn[l]), wv=wv, wg=wg, cv=cv, cg=cg, bv=bv, bg=bg,
        wd=w_down[l].astype(BF16))


def _layer(x, mem, p, g_final, final):
    b, t, _ = x.shape
    qkv_na, qkv_dn, z, gates = _in_proj(x, p["norm_mix"], p["w_main"], p["w_gate"], p["conv_qkv"], p["prm"])
    y_na = _na(qkv_na, p["na_bias"], b, t)
    y_dn = _dn(qkv_dn, z, gates, p["norm_o"], b, t)
    kmem, vmem = _mem_kv(mem, p["norm_mem"], p["w_xkv"])
    x = _mix_xattn(x, y_na, y_dn, p["w_out"], p["norm_x"], p["w_xq"], kmem, vmem, p["w_xo"])
    return _ffn(x, p["norm_ffn"], p["wv"], p["wg"], p["cv"], p["cg"], p["bv"], p["bg"], p["wd"], g_final, final)


def kernel(x_prompt, x_sample, mem_prompt, mem_sample, norm_mix, w_in, rpb, conv_qkv, a_log, dt_bias, norm_o, w_out,
           norm_x, norm_mem, w_xq, w_xkv, w_xo, norm_ffn, w_up, conv_ffn, conv_ffn_b, w_down, norm_final):
    layers = [_prep_layer(l, norm_mix, w_in, rpb, conv_qkv, a_log, dt_bias, norm_o, w_out, norm_x, norm_mem, w_xq,
                          w_xkv, w_xo, norm_ffn, w_up, conv_ffn, conv_ffn_b, w_down) for l in range(DEPTH)]
    g_final = norm_final.reshape(1, -1).astype(F32)
    outs = []
    for x, mem in ((x_prompt, mem_prompt), (x_sample, mem_sample)):
        for l in range(DEPTH):
            x = _layer(x, mem, layers[l], g_final, l == DEPTH - 1)
        outs.append(x)
    return tuple(outs)
```

```python
import functools

import numpy as np
import jax
import jax.numpy as jnp
from jax import lax
from jax.experimental import pallas as pl
from jax.experimental.pallas import tpu as pltpu

F32 = jnp.float32
BF16 = jnp.bfloat16

D_MODEL = 1024
DEPTH = 4
HEAD_DIM = 64
N_HEADS_NA = 8
N_HEADS_DN = 8
D_NA = N_HEADS_NA * HEAD_DIM
D_DN = N_HEADS_DN * HEAD_DIM
GRID_W = 64
WIN_H = 8
WIN_W = 16
CHUNK = 64
N_MEM = 256
N_HEADS_X = 4
HEAD_DIM_X = D_MODEL // N_HEADS_X
D_FF = 2816
RMS_EPS = 1e-6
L2_EPS = 1e-6

LANES = 128
MXU_DIM = 256
N_PAIRS = N_HEADS_DN // 2
D_MAIN = 3 * D_NA + 4 * D_DN
N_GATE = 4 * N_HEADS_DN
FF_CHUNK = 256
N_FF_CHUNKS = D_FF // FF_CHUNK
FFN_AHEAD = 2
HALO = 8
NA_ROWS = 8
NA_AHEAD = 4
DN_GROUP = 8
VMEM_LIMIT = 56 * 1024 * 1024


def _token_block(t):
    return min(1024, t)


def _rms(x, g):
    return x * lax.rsqrt(jnp.mean(x * x, axis=-1, keepdims=True) + RMS_EPS) * g


def _sigmoid(x):
    return 1.0 / (1.0 + jnp.exp(-x))


def _split2(x):
    hi = x.astype(BF16)
    lo = (x - hi.astype(F32)).astype(BF16)
    return hi, lo


def _dot(a, b):
    return jnp.dot(a, b, preferred_element_type=F32)


def _dot_nt(a, b):
    return lax.dot_general(a, b, (((1,), (1,)), ((), ())), preferred_element_type=F32)


def _head_block_ones(n):
    r = lax.broadcasted_iota(jnp.int32, (n, n), 0)
    c = lax.broadcasted_iota(jnp.int32, (n, n), 1)
    return ((r // HEAD_DIM) == (c // HEAD_DIM)).astype(BF16)


def _const_spec(shape):
    nd = len(shape)
    return pl.BlockSpec(shape, lambda *_: (0,) * nd)


def _params(sem):
    return pltpu.CompilerParams(dimension_semantics=sem, vmem_limit_bytes=VMEM_LIMIT)


def _halo_specs(tm, t, d):
    nh = tm // HALO
    last = t // HALO - 1
    return [
        pl.BlockSpec((None, tm, d), lambda i, j: (i, j, 0)),
        pl.BlockSpec((None, HALO, d), lambda i, j: (i, jnp.maximum(j * nh - 1, 0), 0)),
        pl.BlockSpec((None, HALO, d), lambda i, j: (i, jnp.minimum((j + 1) * nh, last), 0)),
    ]


def _normed_with_halo(x_ref, xp_ref, xnx_ref, g):
    j = pl.program_id(1)
    has_prev = jnp.where(j > 0, 1.0, 0.0)
    has_next = jnp.where(j < pl.num_programs(1) - 1, 1.0, 0.0)
    return jnp.concatenate([_rms(xp_ref[...], g) * has_prev, _rms(x_ref[...], g), _rms(xnx_ref[...], g) * has_next],
                           axis=0).astype(BF16)


def _token_conv3(h, w, tm):
    ext = tm + 2 * HALO
    hp = pltpu.roll(h, 1, 0)[HALO:HALO + tm]
    hn = pltpu.roll(h, ext - 1, 0)[HALO:HALO + tm]
    return hp * w[0:1] + h[HALO:HALO + tm] * w[1:2] + hn * w[2:3]


def _in_proj_kernel(x_ref, xp_ref, xnx_ref, g_ref, wm_ref, wg_ref, cw_ref, prm_ref, na_ref, dn_ref, z_ref, gate_ref,
                    *, tm):
    xe = _normed_with_halo(x_ref, xp_ref, xnx_ref, g_ref[...])
    xm = xe[HALO:HALO + tm]

    def na_part(j):
        r = _dot(xm, wm_ref[:, j * D_NA:(j + 1) * D_NA])
        if j == 0:
            r = r * (HEAD_DIM ** -0.5)
        na_ref[:, j * D_NA:(j + 1) * D_NA] = r.astype(na_ref.dtype)

    def z_part():
        off = 3 * D_NA + 3 * D_DN
        z_ref[...] = _dot(xm, wm_ref[:, off:off + D_DN]).astype(z_ref.dtype)

    def gate_part():
        raw = _dot(xm, wg_ref[...])
        col = lax.broadcasted_iota(jnp.int32, raw.shape, 1)
        xx = raw + prm_ref[1:2]
        softplus = jnp.maximum(xx, 0.0) + jnp.log(1.0 + jnp.exp(-jnp.abs(xx)))
        gate_ref[...] = jnp.where(col < 2 * N_HEADS_DN, _sigmoid(raw), -jnp.exp(prm_ref[0:1]) * softplus)

    fillers = [functools.partial(na_part, 0), functools.partial(na_part, 1), functools.partial(na_part, 2),
               z_part, gate_part]

    ones_bd = _head_block_ones(MXU_DIM)
    n_blocks = 3 * D_DN // MXU_DIM

    def dn_proj(c):
        off = 3 * D_NA + c * MXU_DIM
        return _dot(xe, wm_ref[:, off:off + MXU_DIM])

    h_next = dn_proj(0)
    for c in range(n_blocks):
        h = h_next
        if c + 1 < n_blocks:
            h_next = dn_proj(c + 1)
        if fillers:
            fillers.pop(0)()
        cols = slice(c * MXU_DIM, (c + 1) * MXU_DIM)
        y = _token_conv3(h, cw_ref[:, cols], tm)
        y = y * _sigmoid(y)
        if c < 2 * D_DN // MXU_DIM:
            y = y * lax.rsqrt(_dot((y * y).astype(BF16), ones_bd) + L2_EPS)
        if c < D_DN // MXU_DIM:
            y = y * (HEAD_DIM ** -0.5)
        dn_ref[:, cols] = y.astype(dn_ref.dtype)
    for f in fillers:
        f()


def _in_proj(x, g, w_main, w_gate, conv_w, prm):
    b, t, _ = x.shape
    tm = _token_block(t)
    tok = lambda d: pl.BlockSpec((None, tm, d), lambda i, j: (i, j, 0))
    return pl.pallas_call(
        functools.partial(_in_proj_kernel, tm=tm),
        grid=(b, t // tm),
        in_specs=_halo_specs(tm, t, D_MODEL) + [
            _const_spec((1, D_MODEL)),
            _const_spec((D_MODEL, D_MAIN)),
            _const_spec((D_MODEL, LANES)),
            _const_spec((3, 3 * D_DN)),
            _const_spec((2, LANES)),
        ],
        out_specs=[tok(3 * D_NA), tok(3 * D_DN), tok(D_DN), tok(LANES)],
        out_shape=[
            jax.ShapeDtypeStruct((b, t, 3 * D_NA), BF16),
            jax.ShapeDtypeStruct((b, t, 3 * D_DN), BF16),
            jax.ShapeDtypeStruct((b, t, D_DN), BF16),
            jax.ShapeDtypeStruct((b, t, LANES), F32),
        ],
        compiler_params=_params(("parallel", "parallel")),
        name="in_proj",
    )(x, x, x, g, w_main, w_gate, conv_w, prm)


def _na_bias_tables(rpb_l):
    c = np.arange(GRID_W)
    cs = np.clip(c - WIN_W // 2, 0, GRID_W - WIN_W)
    valid = (c[None, :] >= cs[:, None]) & (c[None, :] < cs[:, None] + WIN_W)
    coff = np.clip(c[None, :] - c[:, None] + (WIN_W - 1), 0, 2 * WIN_W - 2)
    roff = np.arange(WIN_H)[None, :] + (WIN_H - 1) - np.arange(WIN_H)[:, None]
    select = np.asarray(coff[..., None] == np.arange(2 * WIN_W - 1), np.float32)
    b = jnp.einsum("vhib,qkb->vhqik", jnp.transpose(rpb_l[:, roff], (1, 0, 2, 3)), select,
                   precision=lax.Precision.HIGHEST)
    b = jnp.where(valid[None, None, :, None, :], b, -jnp.inf)
    return b.reshape(WIN_H, N_PAIRS, 2 * GRID_W, WIN_H * GRID_W).astype(F32)


def _na_kernel(q_ref, k_ref, v_ref, bias_ref, o_ref, *, rows):
    step = pl.program_id(1)
    nk = WIN_H * GRID_W
    lane = lax.broadcasted_iota(jnp.int32, (GRID_W, LANES), 1)
    low = lane < HEAD_DIM

    problems = []
    for rr in range(NA_ROWS):
        r = step * NA_ROWS + rr
        rs = jnp.clip(r - WIN_H // 2, 0, rows - WIN_H)
        start = pl.multiple_of(rs * GRID_W, GRID_W)
        for p in range(N_PAIRS):
            problems.append((rr, p, r - rs, start))

    def scores(rr, p, var, start):
        cols = slice(p * LANES, (p + 1) * LANES)
        q2 = q_ref[rr * GRID_W:(rr + 1) * GRID_W, cols].astype(F32)
        qs = jnp.concatenate([jnp.where(low, q2, 0.0), jnp.where(low, 0.0, q2)], axis=0).astype(BF16)
        return _dot_nt(qs, k_ref[pl.ds(start, nk), cols]) + bias_ref[var, p]

    pending = [scores(*prob) for prob in problems[:NA_AHEAD]]
    for i, (rr, p, var, start) in enumerate(problems):
        s = pending.pop(0)
        if i + NA_AHEAD < len(problems):
            pending.append(scores(*problems[i + NA_AHEAD]))
        e = jnp.exp(s - jnp.max(s, axis=-1, keepdims=True))
        l = jnp.sum(e, axis=-1, keepdims=True)
        cols = slice(p * LANES, (p + 1) * LANES)
        o = _dot(e.astype(BF16), v_ref[pl.ds(start, nk), cols]) * (1.0 / l)
        o_ref[rr * GRID_W:(rr + 1) * GRID_W, cols] = jnp.where(low, o[:GRID_W], o[GRID_W:]).astype(o_ref.dtype)


def _na(qkv, bias, b, t):
    rows = t // GRID_W
    blk = NA_ROWS * GRID_W
    return pl.pallas_call(
        functools.partial(_na_kernel, rows=rows),
        grid=(b, rows // NA_ROWS),
        in_specs=[
            pl.BlockSpec((None, blk, D_NA), lambda i, r: (i, r, 0)),
            pl.BlockSpec((None, t, D_NA), lambda i, r: (i, 0, 1)),
            pl.BlockSpec((None, t, D_NA), lambda i, r: (i, 0, 2)),
            _const_spec((WIN_H, N_PAIRS, 2 * GRID_W, WIN_H * GRID_W)),
        ],
        out_specs=pl.BlockSpec((None, blk, D_NA), lambda i, r: (i, r, 0)),
        out_shape=jax.ShapeDtypeStruct((b, t, D_NA), BF16),
        compiler_params=_params(("parallel", "arbitrary")),
        name="na",
    )(qkv, qkv, qkv, bias)


def _block_diag(x, low):
    return jnp.concatenate([jnp.where(low, x, 0.0), jnp.where(low, 0.0, x)], axis=0)


def _mm(a, b):
    return _dot(a.astype(BF16), b.astype(BF16))


def _neumann(nmats, low, eye2, between):
    ps = [_mm(nm, _block_diag(nm, low)) for nm in nmats]
    ts = [eye2 + nm for nm in nmats]
    between()
    for _ in range(4):
        outs = [_mm(p, jnp.concatenate([_block_diag(p, low), _block_diag(tt, low)], axis=1)) for p, tt in zip(ps, ts)]
        ps = [o[:, :LANES] for o in outs]
        ts = [tt + o[:, LANES:] for tt, o in zip(ts, outs)]
        between()
    res = [tt + _mm(p, _block_diag(tt, low)) for p, tt in zip(ps, ts)]
    between()
    return res


def _dn_kernel(q_ref, k_ref, v_ref, gate_ref, z_ref, normo_ref, o_ref, m_s, n_s, p_s, r_s, gl_s, out_s,
               *, t, n_items):
    n = t // CHUNK
    group = min(DN_GROUP, n)
    grp_rows = group * CHUNK
    step = pl.program_id(0)
    cur = step % 2
    prev = 1 - cur
    pair = jnp.minimum(step, n_items - 1) % N_PAIRS

    @pl.when(step == 0)
    def _():
        m_s[1] = jnp.zeros(m_s.shape[1:], m_s.dtype)
        n_s[1] = jnp.zeros(n_s.shape[1:], n_s.dtype)
        p_s[1] = jnp.zeros(p_s.shape[1:], p_s.dtype)
        r_s[1] = jnp.zeros(r_s.shape[1:], r_s.dtype)
        gl_s[1] = jnp.zeros(gl_s.shape[1:], gl_s.dtype)

    er = lax.broadcasted_iota(jnp.int32, (LANES, 4 * LANES), 0)
    ec = lax.broadcasted_iota(jnp.int32, (LANES, 4 * LANES), 1)
    src = (ec // LANES) * N_HEADS_DN + 2 * pair + (ec % LANES) // HEAD_DIM
    spread = (er == src).astype(BF16)
    spread2 = jnp.concatenate([spread, spread], axis=0)

    lane = lax.broadcasted_iota(jnp.int32, (CHUNK, LANES), 1)
    low = lane < HEAD_DIM
    ri = lax.broadcasted_iota(jnp.int32, (CHUNK, LANES), 0)
    ci = lane % HEAD_DIM
    eye2 = (ri == ci).astype(F32)
    ti = lax.broadcasted_iota(jnp.int32, (CHUNK, CHUNK), 0)
    tj = lax.broadcasted_iota(jnp.int32, (CHUNK, CHUNK), 1)
    neg_ones = jnp.full((CHUNK, CHUNK), -1.0, F32)

    dirs = []
    for tri, keep, upto, strict, last in (
            ((tj <= ti), ci <= ri, ci >= ri, ci < ri, CHUNK - 1),
            ((tj >= ti), ci >= ri, ci <= ri, ci > ri, 0)):
        trif = tri.astype(F32)
        lhs = jnp.concatenate([trif, neg_ones, trif, neg_ones], axis=1).astype(BF16)
        dirs.append((lhs, keep, upto, strict, last))

    def body(grp, states):
        states = list(states)
        done = [0]

        def recurrence_steps(count):
            for _ in range(count):
                k = grp * group + done[0]
                done[0] += 1
                for d, c in enumerate((k, n - 1 - k)):
                    rows = pl.ds(pl.multiple_of(c * CHUNK, CHUNK), CHUNK)
                    s2 = states[d]
                    lhs = jnp.concatenate([m_s[prev, d, c], p_s[prev, d, rows, :]], axis=0)
                    ms_ps = _dot(lhs, _block_diag(s2, low).astype(BF16))
                    out_s[d, rows, :] = ms_ps[CHUNK:] + r_s[prev, d, rows, :]
                    states[d] = s2 * gl_s[prev, d, c][0:1, :] + n_s[prev, d, c].astype(F32) - ms_ps[:CHUNK]

        rounds = 10
        schedule = iter([group // rounds + (1 if i < group % rounds else 0) for i in range(rounds)])
        between = lambda: recurrence_steps(next(schedule))

        r0 = pl.multiple_of(grp * grp_rows, grp_rows)
        rows_g = pl.ds(r0, grp_rows)
        kf = k_ref[rows_g, :].astype(F32)
        qf = q_ref[rows_g, :].astype(F32)
        vf = v_ref[rows_g, :].astype(F32)
        ghi, glo = _split2(gate_ref[rows_g, :])
        sp = _dot(jnp.concatenate([ghi, glo], axis=1), spread2)

        chunks = []
        for u in range(group):
            sl = slice(u * CHUNK, (u + 1) * CHUNK)
            kc, qc, vc = kf[sl], qf[sl], vf[sl]
            kq = jnp.concatenate([kc, qc], axis=0).astype(BF16)
            gq = _dot_nt(kq, _block_diag(kc, low).astype(BF16))
            chunks.append((kc, qc, vc, gq[:CHUNK], gq[CHUNK:], sp[sl]))
        between()

        dgs = []
        for kc, qc, vc, gram, qk, spc in chunks:
            for d, (lhs, keep, upto, strict, last) in enumerate(dirs):
                ghi2, glo2 = _split2(spc[:, (2 + d) * LANES:(3 + d) * LANES])
                zero = jnp.zeros_like(ghi2)
                rhs = jnp.concatenate([
                    jnp.concatenate([ghi2, jnp.where(upto, ghi2, zero), glo2, jnp.where(upto, glo2, zero)], axis=0),
                    jnp.concatenate([ghi2, zero, glo2, zero], axis=0)], axis=1)
                dgs.append(_dot(lhs, rhs))
        between()

        probs = []
        nmats = []
        it = iter(dgs)
        for kc, qc, vc, gram, qk, spc in chunks:
            for d, (lhs, keep, upto, strict, last) in enumerate(dirs):
                dg = next(it)
                beta = spc[:, d * LANES:(d + 1) * LANES]
                delta = dg[:, :LANES]
                gcol = dg[:, LANES:]
                decay = jnp.where(keep, jnp.exp(jnp.minimum(delta, 0.0)), 0.0)
                gamma = jnp.exp(gcol)
                glast = gcol[last:last + 1, :]
                nmats.append(jnp.where(strict, -(beta * gram * decay), 0.0))
                kd = kc * jnp.exp(glast - gcol)
                kd_t = jnp.transpose(jnp.concatenate([kd, jnp.zeros_like(kd)], axis=0))
                kdt = kd_t[:HEAD_DIM] + pltpu.roll(kd_t[HEAD_DIM:], HEAD_DIM, 1)
                probs.append((kc, qc, vc, beta, gamma, glast, qk * decay, kdt))
        tinvs = _neumann(nmats, low, eye2, between)

        uws = []
        for (kc, qc, vc, beta, gamma, glast, qkd, kdt), tinv in zip(probs, tinvs):
            rhs = jnp.concatenate([_block_diag(kc * (beta * gamma), low), _block_diag(vc * beta, low)], axis=1)
            uws.append(_mm(tinv, rhs))
        between()

        mnprs = []
        for (kc, qc, vc, beta, gamma, glast, qkd, kdt), wu in zip(probs, uws):
            lhs = jnp.concatenate([kdt, qkd], axis=0)
            rhs = jnp.concatenate([_block_diag(wu[:, :LANES], low), _block_diag(wu[:, LANES:], low)], axis=1)
            mnprs.append(_mm(lhs, rhs))
        between()

        idx = 0
        for u in range(group):
            c = grp * group + u
            rows = pl.ds(pl.multiple_of(r0 + u * CHUNK, CHUNK), CHUNK)
            for d in range(2):
                kc, qc, vc, beta, gamma, glast, qkd, kdt = probs[idx]
                mn, pr = mnprs[idx][:CHUNK], mnprs[idx][CHUNK:]
                m_s[cur, d, c] = mn[:, :LANES].astype(BF16)
                n_s[cur, d, c] = mn[:, LANES:].astype(BF16)
                p_s[cur, d, rows, :] = (qc * gamma - pr[:, :LANES]).astype(BF16)
                r_s[cur, d, rows, :] = pr[:, LANES:]
                gl_s[cur, d, c] = jnp.broadcast_to(jnp.exp(glast), (8, LANES))
                idx += 1
        assert done[0] == group
        return tuple(states)

    zero_state = jnp.zeros((CHUNK, LANES), F32)
    lax.fori_loop(0, n // group, body, (zero_state, zero_state))

    o = out_s[0] + out_s[1]
    sq = (o * o).astype(BF16)
    ss = _dot(jnp.concatenate([sq[:t // 2], sq[t // 2:]], axis=1), _head_block_ones(MXU_DIM))
    ms = jnp.concatenate([ss[:, :LANES], ss[:, LANES:]], axis=0) * (1.0 / HEAD_DIM)
    o = o * lax.rsqrt(ms + RMS_EPS) * normo_ref[...]
    z = z_ref[...].astype(F32)
    o_ref[...] = (o * (z * _sigmoid(z))).astype(o_ref.dtype)


def _dn(qkv, z, gates, normo2, b, t):
    n = t // CHUNK
    n_items = b * N_PAIRS

    def cur_spec(base):
        def index(s):
            i = jnp.minimum(s, n_items - 1)
            return (i // N_PAIRS, 0, base + i % N_PAIRS)
        return pl.BlockSpec((None, t, LANES), index)

    def lag_index(s):
        i = jnp.maximum(s - 1, 0)
        return (i // N_PAIRS, 0, i % N_PAIRS)

    return pl.pallas_call(
        functools.partial(_dn_kernel, t=t, n_items=n_items),
        grid=(n_items + 1,),
        in_specs=[
            cur_spec(0), cur_spec(N_PAIRS), cur_spec(2 * N_PAIRS),
            pl.BlockSpec((None, t, LANES), lambda s: (jnp.minimum(s, n_items - 1) // N_PAIRS, 0, 0)),
            pl.BlockSpec((None, t, LANES), lag_index),
            _const_spec((1, LANES)),
        ],
        out_specs=pl.BlockSpec((None, t, LANES), lag_index),
        out_shape=jax.ShapeDtypeStruct((b, t, D_DN), BF16),
        scratch_shapes=[
            pltpu.VMEM((2, 2, n, CHUNK, LANES), BF16),
            pltpu.VMEM((2, 2, n, CHUNK, LANES), BF16),
            pltpu.VMEM((2, 2, t, LANES), BF16),
            pltpu.VMEM((2, 2, t, LANES), F32),
            pltpu.VMEM((2, 2, n, 8, LANES), F32),
            pltpu.VMEM((2, t, LANES), F32),
        ],
        compiler_params=_params(("arbitrary",)),
        name="dn",
    )(qkv, qkv, qkv, gates, z, normo2)


def _mem_kv_kernel(m_ref, g_ref, w_ref, k_ref, v_ref):
    mn = _rms(m_ref[...], g_ref[...]).astype(BF16)
    k_ref[...] = _dot(mn, w_ref[:, :D_MODEL]).astype(k_ref.dtype)
    v_ref[...] = _dot(mn, w_ref[:, D_MODEL:]).astype(v_ref.dtype)


def _mem_kv(mem, g, w_kv):
    b = mem.shape[0]
    blk = pl.BlockSpec((None, N_MEM, D_MODEL), lambda i: (i, 0, 0))
    return pl.pallas_call(
        _mem_kv_kernel,
        grid=(b,),
        in_specs=[blk, _const_spec((1, D_MODEL)), _const_spec((D_MODEL, 2 * D_MODEL))],
        out_specs=[blk, blk],
        out_shape=[jax.ShapeDtypeStruct((b, N_MEM, D_MODEL), BF16)] * 2,
        compiler_params=_params(("parallel",)),
        name="mem_kv",
    )(mem, g, w_kv)


def _mix_xattn_kernel(x_ref, yna_ref, ydn_ref, wout_ref, g_ref, wq_ref, k_ref, v_ref, wo_ref, o_ref):
    x = x_ref[...] + _dot(yna_ref[...], wout_ref[:D_NA, :]) + _dot(ydn_ref[...], wout_ref[D_NA:, :])
    xn = _rms(x, g_ref[...]).astype(BF16)
    q = (_dot(xn, wq_ref[...]) * (HEAD_DIM_X ** -0.5)).astype(BF16)
    head_cols = [slice(h * HEAD_DIM_X, (h + 1) * HEAD_DIM_X) for h in range(N_HEADS_X)]
    scores = [_dot_nt(q[:, cols], k_ref[:, cols]) for cols in head_cols]
    probs = []
    for s in scores:
        e = jnp.exp(s - jnp.max(s, axis=-1, keepdims=True))
        probs.append((e.astype(BF16), jnp.sum(e, axis=-1, keepdims=True)))
    heads = [(_dot(e, v_ref[:, cols]) * (1.0 / l)).astype(BF16) for (e, l), cols in zip(probs, head_cols)]
    o_ref[...] = x + _dot(jnp.concatenate(heads, axis=-1), wo_ref[...])


def _mix_xattn(x, y_na, y_dn, w_out, g, w_q, kmem, vmem, w_o):
    b, t, _ = x.shape
    tm = _token_block(t)
    tok = lambda d: pl.BlockSpec((None, tm, d), lambda i, j: (i, j, 0))
    memspec = pl.BlockSpec((None, N_MEM, D_MODEL), lambda i, j: (i, 0, 0))
    return pl.pallas_call(
        _mix_xattn_kernel,
        grid=(b, t // tm),
        in_specs=[
            tok(D_MODEL), tok(D_NA), tok(D_DN),
            _const_spec((D_NA + D_DN, D_MODEL)),
            _const_spec((1, D_MODEL)),
            _const_spec((D_MODEL, D_MODEL)),
            memspec, memspec,
            _const_spec((D_MODEL, D_MODEL)),
        ],
        out_specs=tok(D_MODEL),
        out_shape=jax.ShapeDtypeStruct((b, t, D_MODEL), F32),
        compiler_params=_params(("parallel", "parallel")),
        name="mix_xattn",
    )(x, y_na, y_dn, w_out, g, w_q, kmem, vmem, w_o)


def _ffn_kernel(x_ref, xp_ref, xnx_ref, g_ref, wv_ref, wg_ref, cv_ref, cg_ref, bv_ref, bg_ref, wd_ref, gf_ref, o_ref,
                *, tm, final):
    xe = _normed_with_halo(x_ref, xp_ref, xnx_ref, g_ref[...])
    ups = [(_dot(xe, wv_ref[c]), _dot(xe, wg_ref[c])) for c in range(FFN_AHEAD)]
    acts = []
    for c in range(N_FF_CHUNKS):
        hv, hg = ups.pop(0)
        if c + FFN_AHEAD < N_FF_CHUNKS:
            ups.append((_dot(xe, wv_ref[c + FFN_AHEAD]), _dot(xe, wg_ref[c + FFN_AHEAD])))
        val = _token_conv3(hv, cv_ref[c], tm) + bv_ref[c]
        gate = _token_conv3(hg, cg_ref[c], tm) + bg_ref[c]
        acts.append((gate * _sigmoid(gate) * val).astype(BF16))
    acc = x_ref[...] + _dot(jnp.concatenate(acts, axis=1), wd_ref[...])
    if final:
        acc = _rms(acc, gf_ref[...])
    o_ref[...] = acc


def _ffn(x, g, wv, wg, cv, cg, bv, bg, wd, g_final, final):
    b, t, _ = x.shape
    tm = _token_block(t)
    return pl.pallas_call(
        functools.partial(_ffn_kernel, tm=tm, final=final),
        grid=(b, t // tm),
        in_specs=_halo_specs(tm, t, D_MODEL) + [
            _const_spec((1, D_MODEL)),
            _const_spec((N_FF_CHUNKS, D_MODEL, FF_CHUNK)),
            _const_spec((N_FF_CHUNKS, D_MODEL, FF_CHUNK)),
            _const_spec((N_FF_CHUNKS, 3, FF_CHUNK)),
            _const_spec((N_FF_CHUNKS, 3, FF_CHUNK)),
            _const_spec((N_FF_CHUNKS, 1, FF_CHUNK)),
            _const_spec((N_FF_CHUNKS, 1, FF_CHUNK)),
            _const_spec((D_FF, D_MODEL)),
            _const_spec((1, D_MODEL)),
        ],
        out_specs=pl.BlockSpec((None, tm, D_MODEL), lambda i, j: (i, j, 0)),
        out_shape=jax.ShapeDtypeStruct((b, t, D_MODEL), F32),
        compiler_params=_params(("parallel", "parallel")),
        name="ffn",
    )(x, x, x, g, wv, wg, cv, cg, bv, bg, wd, g_final)


def _chunk_cols(w):
    rows = w.shape[0]
    parts = w.reshape(rows, 2, N_FF_CHUNKS, FF_CHUNK)
    return jnp.transpose(parts[:, 0], (1, 0, 2)), jnp.transpose(parts[:, 1], (1, 0, 2))


def _prep_layer(l, norm_mix, w_in, rpb, conv_qkv, a_log, dt_bias, norm_o, w_out, norm_x, norm_mem, w_xq, w_xkv, w_xo,
                norm_ffn, w_up, conv_ffn, conv_ffn_b, w_down):
    row = lambda v: v.reshape(1, -1).astype(F32)
    w_gate = jnp.pad(w_in[l][:, D_MAIN:], ((0, 0), (0, LANES - N_GATE))).astype(BF16)
    pad = jnp.zeros((2 * N_HEADS_DN,), F32)
    tail = jnp.zeros((LANES - N_GATE,), F32)
    prm = jnp.stack([jnp.concatenate([pad, a_log[l].reshape(-1), tail]),
                     jnp.concatenate([pad, dt_bias[l].reshape(-1), tail])])
    wv, wg = _chunk_cols(w_up[l].astype(BF16))
    cv, cg = _chunk_cols(conv_ffn[l])
    bv, bg = _chunk_cols(conv_ffn_b[l].reshape(1, -1))
    return dict(
        norm_mix=row(norm_mix[l]), w_main=w_in[l][:, :D_MAIN].astype(BF16), w_gate=w_gate,
        na_bias=_na_bias_tables(rpb[l]), conv_qkv=conv_qkv[l], prm=prm,
        norm_o=row(jnp.concatenate([norm_o[l], norm_o[l]])), w_out=w_out[l].astype(BF16),
        norm_x=row(norm_x[l]), norm_mem=row(norm_mem[l]), w_xq=w_xq[l].astype(BF16), w_xkv=w_xkv[l].astype(BF16),
        w_xo=w_xo[l].astype(BF16), norm_ffn=row(norm_ffn[l]), wv=wv, wg=wg, cv=cv, cg=cg, bv=bv, bg=bg,
        wd=w_down[l].astype(BF16))


def _layer(x, mem, p, g_final, final):
    b, t, _ = x.shape
    qkv_na, qkv_dn, z, gates = _in_proj(x, p["norm_mix"], p["w_main"], p["w_gate"], p["conv_qkv"], p["prm"])
    y_na = _na(qkv_na, p["na_bias"], b, t)
    y_dn = _dn(qkv_dn, z, gates, p["norm_o"], b, t)
    kmem, vmem = _mem_kv(mem, p["norm_mem"], p["w_xkv"])
    x = _mix_xattn(x, y_na, y_dn, p["w_out"], p["norm_x"], p["w_xq"], kmem, vmem, p["w_xo"])
    return _ffn(x, p["norm_ffn"], p["wv"], p["wg"], p["cv"], p["cg"], p["bv"], p["bg"], p["wd"], g_final, final)


def kernel(x_prompt, x_sample, mem_prompt, mem_sample, norm_mix, w_in, rpb, conv_qkv, a_log, dt_bias, norm_o, w_out,
           norm_x, norm_mem, w_xq, w_xkv, w_xo, norm_ffn, w_up, conv_ffn, conv_ffn_b, w_down, norm_final):
    layers = [_prep_layer(l, norm_mix, w_in, rpb, conv_qkv, a_log, dt_bias, norm_o, w_out, norm_x, norm_mem, w_xq,
                          w_xkv, w_xo, norm_ffn, w_up, conv_ffn, conv_ffn_b, w_down) for l in range(DEPTH)]
    g_final = norm_final.reshape(1, -1).astype(F32)
    outs = []
    for x, mem in ((x_prompt, mem_prompt), (x_sample, mem_sample)):
        for l in range(DEPTH):
            x = _layer(x, mem, layers[l], g_final, l == DEPTH - 1)
        outs.append(x)
    return tuple(outs)
```

```python
import functools

import numpy as np
import jax
import jax.numpy as jnp
from jax import lax
from jax.experimental import pallas as pl
from jax.experimental.pallas import tpu as pltpu

F32 = jnp.float32
BF16 = jnp.bfloat16

D_MODEL = 1024
DEPTH = 4
HEAD_DIM = 64
N_HEADS_NA = 8
N_HEADS_DN = 8
D_NA = N_HEADS_NA * HEAD_DIM
D_DN = N_HEADS_DN * HEAD_DIM
GRID_W = 64
WIN_H = 8
WIN_W = 16
CHUNK = 64
N_MEM = 256
N_HEADS_X = 4
HEAD_DIM_X = D_MODEL // N_HEADS_X
D_FF = 2816
RMS_EPS = 1e-6
L2_EPS = 1e-6

LANES = 128
MXU_DIM = 256
N_PAIRS = N_HEADS_DN // 2
D_MAIN = 3 * D_NA + 4 * D_DN
N_GATE = 4 * N_HEADS_DN
FF_CHUNK = 256
N_FF_CHUNKS = D_FF // FF_CHUNK
FFN_AHEAD = 2
HALO = 8
NA_ROWS = 8
NA_AHEAD = 4
DN_GROUP = 8
VMEM_LIMIT = 56 * 1024 * 1024


def _token_block(t):
    return min(1024, t)


def _rms(x, g):
    return x * lax.rsqrt(jnp.mean(x * x, axis=-1, keepdims=True) + RMS_EPS) * g


def _sigmoid(x):
    return 1.0 / (1.0 + jnp.exp(-x))


def _split2(x):
    hi = x.astype(BF16)
    lo = (x - hi.astype(F32)).astype(BF16)
    return hi, lo


def _dot(a, b):
    return jnp.dot(a, b, preferred_element_type=F32)


def _dot_nt(a, b):
    return lax.dot_general(a, b, (((1,), (1,)), ((), ())), preferred_element_type=F32)


def _head_block_ones(n):
    r = lax.broadcasted_iota(jnp.int32, (n, n), 0)
    c = lax.broadcasted_iota(jnp.int32, (n, n), 1)
    return ((r // HEAD_DIM) == (c // HEAD_DIM)).astype(BF16)


def _const_spec(shape):
    nd = len(shape)
    return pl.BlockSpec(shape, lambda *_: (0,) * nd)


def _params(sem):
    return pltpu.CompilerParams(dimension_semantics=sem, vmem_limit_bytes=VMEM_LIMIT)


def _halo_specs(tm, t, d):
    nh = tm // HALO
    last = t // HALO - 1
    return [
        pl.BlockSpec((None, tm, d), lambda i, j: (i, j, 0)),
        pl.BlockSpec((None, HALO, d), lambda i, j: (i, jnp.maximum(j * nh - 1, 0), 0)),
        pl.BlockSpec((None, HALO, d), lambda i, j: (i, jnp.minimum((j + 1) * nh, last), 0)),
    ]


def _normed_with_halo(x_ref, xp_ref, xnx_ref, g):
    j = pl.program_id(1)
    has_prev = jnp.where(j > 0, 1.0, 0.0)
    has_next = jnp.where(j < pl.num_programs(1) - 1, 1.0, 0.0)
    return jnp.concatenate([_rms(xp_ref[...], g) * has_prev, _rms(x_ref[...], g), _rms(xnx_ref[...], g) * has_next],
                           axis=0).astype(BF16)


def _token_conv3(h, w, tm):
    ext = tm + 2 * HALO
    hp = pltpu.roll(h, 1, 0)[HALO:HALO + tm]
    hn = pltpu.roll(h, ext - 1, 0)[HALO:HALO + tm]
    return hp * w[0:1] + h[HALO:HALO + tm] * w[1:2] + hn * w[2:3]


def _in_proj_kernel(x_ref, xp_ref, xnx_ref, g_ref, wm_ref, wg_ref, cw_ref, prm_ref, na_ref, dn_ref, z_ref, gate_ref,
                    *, tm):
    xe = _normed_with_halo(x_ref, xp_ref, xnx_ref, g_ref[...])
    xm = xe[HALO:HALO + tm]

    def na_part(j):
        r = _dot(xm, wm_ref[:, j * D_NA:(j + 1) * D_NA])
        if j == 0:
            r = r * (HEAD_DIM ** -0.5)
        na_ref[:, j * D_NA:(j + 1) * D_NA] = r.astype(na_ref.dtype)

    def z_part():
        off = 3 * D_NA + 3 * D_DN
        z_ref[...] = _dot(xm, wm_ref[:, off:off + D_DN]).astype(z_ref.dtype)

    def gate_part():
        raw = _dot(xm, wg_ref[...])
        col = lax.broadcasted_iota(jnp.int32, raw.shape, 1)
        xx = raw + prm_ref[1:2]
        softplus = jnp.maximum(xx, 0.0) + jnp.log(1.0 + jnp.exp(-jnp.abs(xx)))
        gate_ref[...] = jnp.where(col < 2 * N_HEADS_DN, _sigmoid(raw), -jnp.exp(prm_ref[0:1]) * softplus)

    fillers = [functools.partial(na_part, 0), functools.partial(na_part, 1), functools.partial(na_part, 2),
               z_part, gate_part]

    ones_bd = _head_block_ones(MXU_DIM)
    n_blocks = 3 * D_DN // MXU_DIM

    def dn_proj(c):
        off = 3 * D_NA + c * MXU_DIM
        return _dot(xe, wm_ref[:, off:off + MXU_DIM])

    h_next = dn_proj(0)
    for c in range(n_blocks):
        h = h_next
        if c + 1 < n_blocks:
            h_next = dn_proj(c + 1)
        if fillers:
            fillers.pop(0)()
        cols = slice(c * MXU_DIM, (c + 1) * MXU_DIM)
        y = _token_conv3(h, cw_ref[:, cols], tm)
        y = y * _sigmoid(y)
        if c < 2 * D_DN // MXU_DIM:
            y = y * lax.rsqrt(_dot((y * y).astype(BF16), ones_bd) + L2_EPS)
        if c < D_DN // MXU_DIM:
            y = y * (HEAD_DIM ** -0.5)
        dn_ref[:, cols] = y.astype(dn_ref.dtype)
    for f in fillers:
        f()


def _in_proj(x, g, w_main, w_gate, conv_w, prm):
    b, t, _ = x.shape
    tm = _token_block(t)
    tok = lambda d: pl.BlockSpec((None, tm, d), lambda i, j: (i, j, 0))
    return pl.pallas_call(
        functools.partial(_in_proj_kernel, tm=tm),
        grid=(b, t // tm),
        in_specs=_halo_specs(tm, t, D_MODEL) + [
            _const_spec((1, D_MODEL)),
            _const_spec((D_MODEL, D_MAIN)),
            _const_spec((D_MODEL, LANES)),
            _const_spec((3, 3 * D_DN)),
            _const_spec((2, LANES)),
        ],
        out_specs=[tok(3 * D_NA), tok(3 * D_DN), tok(D_DN), tok(LANES)],
        out_shape=[
            jax.ShapeDtypeStruct((b, t, 3 * D_NA), BF16),
            jax.ShapeDtypeStruct((b, t, 3 * D_DN), BF16),
            jax.ShapeDtypeStruct((b, t, D_DN), BF16),
            jax.ShapeDtypeStruct((b, t, LANES), F32),
        ],
        compiler_params=_params(("parallel", "parallel")),
        name="in_proj",
    )(x, x, x, g, w_main, w_gate, conv_w, prm)


def _na_bias_tables(rpb_l):
    c = np.arange(GRID_W)
    cs = np.clip(c - WIN_W // 2, 0, GRID_W - WIN_W)
    valid = (c[None, :] >= cs[:, None]) & (c[None, :] < cs[:, None] + WIN_W)
    coff = np.clip(c[None, :] - c[:, None] + (WIN_W - 1), 0, 2 * WIN_W - 2)
    roff = np.arange(WIN_H)[None, :] + (WIN_H - 1) - np.arange(WIN_H)[:, None]
    select = np.asarray(coff[..., None] == np.arange(2 * WIN_W - 1), np.float32)
    b = jnp.einsum("vhib,qkb->vhqik", jnp.transpose(rpb_l[:, roff], (1, 0, 2, 3)), select,
                   precision=lax.Precision.HIGHEST)
    b = jnp.where(valid[None, None, :, None, :], b, -jnp.inf)
    return b.reshape(WIN_H, N_PAIRS, 2 * GRID_W, WIN_H * GRID_W).astype(F32)


def _na_kernel(q_ref, k_ref, v_ref, bias_ref, o_ref, *, rows):
    step = pl.program_id(1)
    nk = WIN_H * GRID_W
    lane = lax.broadcasted_iota(jnp.int32, (GRID_W, LANES), 1)
    low = lane < HEAD_DIM

    problems = []
    for rr in range(NA_ROWS):
        r = step * NA_ROWS + rr
        rs = jnp.clip(r - WIN_H // 2, 0, rows - WIN_H)
        start = pl.multiple_of(rs * GRID_W, GRID_W)
        for p in range(N_PAIRS):
            problems.append((rr, p, r - rs, start))

    def scores(rr, p, var, start):
        cols = slice(p * LANES, (p + 1) * LANES)
        q2 = q_ref[rr * GRID_W:(rr + 1) * GRID_W, cols].astype(F32)
        qs = jnp.concatenate([jnp.where(low, q2, 0.0), jnp.where(low, 0.0, q2)], axis=0).astype(BF16)
        return _dot_nt(qs, k_ref[pl.ds(start, nk), cols]) + bias_ref[var, p]

    pending = [scores(*prob) for prob in problems[:NA_AHEAD]]
    for i, (rr, p, var, start) in enumerate(problems):
        s = pending.pop(0)
        if i + NA_AHEAD < len(problems):
            pending.append(scores(*problems[i + NA_AHEAD]))
        e = jnp.exp(s - jnp.max(s, axis=-1, keepdims=True))
        l = jnp.sum(e, axis=-1, keepdims=True)
        cols = slice(p * LANES, (p + 1) * LANES)
        o = _dot(e.astype(BF16), v_ref[pl.ds(start, nk), cols]) * (1.0 / l)
        o_ref[rr * GRID_W:(rr + 1) * GRID_W, cols] = jnp.where(low, o[:GRID_W], o[GRID_W:]).astype(o_ref.dtype)


def _na(qkv, bias, b, t):
    rows = t // GRID_W
    blk = NA_ROWS * GRID_W
    return pl.pallas_call(
        functools.partial(_na_kernel, rows=rows),
        grid=(b, rows // NA_ROWS),
        in_specs=[
            pl.BlockSpec((None, blk, D_NA), lambda i, r: (i, r, 0)),
            pl.BlockSpec((None, t, D_NA), lambda i, r: (i, 0, 1)),
            pl.BlockSpec((None, t, D_NA), lambda i, r: (i, 0, 2)),
            _const_spec((WIN_H, N_PAIRS, 2 * GRID_W, WIN_H * GRID_W)),
        ],
        out_specs=pl.BlockSpec((None, blk, D_NA), lambda i, r: (i, r, 0)),
        out_shape=jax.ShapeDtypeStruct((b, t, D_NA), BF16),
        compiler_params=_params(("parallel", "arbitrary")),
        name="na",
    )(qkv, qkv, qkv, bias)


def _block_diag(x, low):
    return jnp.concatenate([jnp.where(low, x, 0.0), jnp.where(low, 0.0, x)], axis=0)


def _mm(a, b):
    return _dot(a.astype(BF16), b.astype(BF16))


def _neumann(nmats, low, eye2, between):
    ps = [_mm(nm, _block_diag(nm, low)) for nm in nmats]
    ts = [eye2 + nm for nm in nmats]
    between()
    for _ in range(4):
        outs = [_mm(p, jnp.concatenate([_block_diag(p, low), _block_diag(tt, low)], axis=1)) for p, tt in zip(ps, ts)]
        ps = [o[:, :LANES] for o in outs]
        ts = [tt + o[:, LANES:] for tt, o in zip(ts, outs)]
        between()
    res = [tt + _mm(p, _block_diag(tt, low)) for p, tt in zip(ps, ts)]
    between()
    return res


def _dn_kernel(q_ref, k_ref, v_ref, gate_ref, z_ref, normo_ref, o_ref, m_s, n_s, p_s, r_s, gl_s, out_s,
               *, t, n_items):
    n = t // CHUNK
    group = min(DN_GROUP, n)
    grp_rows = group * CHUNK
    step = pl.program_id(0)
    cur = step % 2
    prev = 1 - cur
    pair = jnp.minimum(step, n_items - 1) % N_PAIRS

    @pl.when(step == 0)
    def _():
        m_s[1] = jnp.zeros(m_s.shape[1:], m_s.dtype)
        n_s[1] = jnp.zeros(n_s.shape[1:], n_s.dtype)
        p_s[1] = jnp.zeros(p_s.shape[1:], p_s.dtype)
        r_s[1] = jnp.zeros(r_s.shape[1:], r_s.dtype)
        gl_s[1] = jnp.zeros(gl_s.shape[1:], gl_s.dtype)

    er = lax.broadcasted_iota(jnp.int32, (LANES, 4 * LANES), 0)
    ec = lax.broadcasted_iota(jnp.int32, (LANES, 4 * LANES), 1)
    src = (ec // LANES) * N_HEADS_DN + 2 * pair + (ec % LANES) // HEAD_DIM
    spread = (er == src).astype(BF16)
    spread2 = jnp.concatenate([spread, spread], axis=0)

    lane = lax.broadcasted_iota(jnp.int32, (CHUNK, LANES), 1)
    low = lane < HEAD_DIM
    ri = lax.broadcasted_iota(jnp.int32, (CHUNK, LANES), 0)
    ci = lane % HEAD_DIM
    eye2 = (ri == ci).astype(F32)
    ti = lax.broadcasted_iota(jnp.int32, (CHUNK, CHUNK), 0)
    tj = lax.broadcasted_iota(jnp.int32, (CHUNK, CHUNK), 1)
    neg_ones = jnp.full((CHUNK, CHUNK), -1.0, F32)

    dirs = []
    for tri, keep, upto, strict, last in (
            ((tj <= ti), ci <= ri, ci >= ri, ci < ri, CHUNK - 1),
            ((tj >= ti), ci >= ri, ci <= ri, ci > ri, 0)):
        trif = tri.astype(F32)
        lhs = jnp.concatenate([trif, neg_ones, trif, neg_ones], axis=1).astype(BF16)
        dirs.append((lhs, keep, upto, strict, last))

    def body(grp, states):
        states = list(states)
        done = [0]

        def recurrence_steps(count):
            for _ in range(count):
                k = grp * group + done[0]
                done[0] += 1
                for d, c in enumerate((k, n - 1 - k)):
                    rows = pl.ds(pl.multiple_of(c * CHUNK, CHUNK), CHUNK)
                    s2 = states[d]
                    lhs = jnp.concatenate([m_s[prev, d, c], p_s[prev, d, rows, :]], axis=0)
                    ms_ps = _dot(lhs, _block_diag(s2, low).astype(BF16))
                    out_s[d, rows, :] = ms_ps[CHUNK:] + r_s[prev, d, rows, :]
                    states[d] = s2 * gl_s[prev, d, c][0:1, :] + n_s[prev, d, c].astype(F32) - ms_ps[:CHUNK]

        rounds = 10
        schedule = iter([group // rounds + (1 if i < group % rounds else 0) for i in range(rounds)])
        between = lambda: recurrence_steps(next(schedule))

        r0 = pl.multiple_of(grp * grp_rows, grp_rows)
        rows_g = pl.ds(r0, grp_rows)
        kf = k_ref[rows_g, :].astype(F32)
        qf = q_ref[rows_g, :].astype(F32)
        vf = v_ref[rows_g, :].astype(F32)
        ghi, glo = _split2(gate_ref[rows_g, :])
        sp = _dot(jnp.concatenate([ghi, glo], axis=1), spread2)

        chunks = []
        for u in range(group):
            sl = slice(u * CHUNK, (u + 1) * CHUNK)
            kc, qc, vc = kf[sl], qf[sl], vf[sl]
            kq = jnp.concatenate([kc, qc], axis=0).astype(BF16)
            gq = _dot_nt(kq, _block_diag(kc, low).astype(BF16))
            chunks.append((kc, qc, vc, gq[:CHUNK], gq[CHUNK:], sp[sl]))
        between()

        dgs = []
        for kc, qc, vc, gram, qk, spc in chunks:
            for d, (lhs, keep, upto, strict, last) in enumerate(dirs):
                ghi2, glo2 = _split2(spc[:, (2 + d) * LANES:(3 + d) * LANES])
                zero = jnp.zeros_like(ghi2)
                rhs = jnp.concatenate([
                    jnp.concatenate([ghi2, jnp.where(upto, ghi2, zero), glo2, jnp.where(upto, glo2, zero)], axis=0),
                    jnp.concatenate([ghi2, zero, glo2, zero], axis=0)], axis=1)
                dgs.append(_dot(lhs, rhs))
        between()

        probs = []
        nmats = []
        it = iter(dgs)
        for kc, qc, vc, gram, qk, spc in chunks:
            for d, (lhs, keep, upto, strict, last) in enumerate(dirs):
                dg = next(it)
                beta = spc[:, d * LANES:(d + 1) * LANES]
                delta = dg[:, :LANES]
                gcol = dg[:, LANES:]
                decay = jnp.where(keep, jnp.exp(jnp.minimum(delta, 0.0)), 0.0)
                gamma = jnp.exp(gcol)
                glast = gcol[last:last + 1, :]
                nmats.append(jnp.where(strict, -(beta * gram * decay), 0.0))
                kd = kc * jnp.exp(glast - gcol)
                kd_t = jnp.transpose(jnp.concatenate([kd, jnp.zeros_like(kd)], axis=0))
                kdt = kd_t[:HEAD_DIM] + pltpu.roll(kd_t[HEAD_DIM:], HEAD_DIM, 1)
                probs.append((kc, qc, vc, beta, gamma, glast, qk * decay, kdt))
        tinvs = _neumann(nmats, low, eye2, between)

        uws = []
        for (kc, qc, vc, beta, gamma, glast, qkd, kdt), tinv in zip(probs, tinvs):
            rhs = jnp.concatenate([_block_diag(kc * (beta * gamma), low), _block_diag(vc * beta, low)], axis=1)
            uws.append(_mm(tinv, rhs))
        between()

        mnprs = []
        for (kc, qc, vc, beta, gamma, glast, qkd, kdt), wu in zip(probs, uws):
            lhs = jnp.concatenate([kdt, qkd], axis=0)
            rhs = jnp.concatenate([_block_diag(wu[:, :LANES], low), _block_diag(wu[:, LANES:], low)], axis=1)
            mnprs.append(_mm(lhs, rhs))
        between()

        idx = 0
        for u in range(group):
            c = grp * group + u
            rows = pl.ds(pl.multiple_of(r0 + u * CHUNK, CHUNK), CHUNK)
            for d in range(2):
                kc, qc, vc, beta, gamma, glast, qkd, kdt = probs[idx]
                mn, pr = mnprs[idx][:CHUNK], mnprs[idx][CHUNK:]
                m_s[cur, d, c] = mn[:, :LANES].astype(BF16)
                n_s[cur, d, c] = mn[:, LANES:].astype(BF16)
                p_s[cur, d, rows, :] = (qc * gamma - pr[:, :LANES]).astype(BF16)
                r_s[cur, d, rows, :] = pr[:, LANES:]
                gl_s[cur, d, c] = jnp.broadcast_to(jnp.exp(glast), (8, LANES))
                idx += 1
        assert done[0] == group
        return tuple(states)

    zero_state = jnp.zeros((CHUNK, LANES), F32)
    lax.fori_loop(0, n // group, body, (zero_state, zero_state), unroll=2)

    o = out_s[0] + out_s[1]
    sq = (o * o).astype(BF16)
    ss = _dot(jnp.concatenate([sq[:t // 2], sq[t // 2:]], axis=1), _head_block_ones(MXU_DIM))
    ms = jnp.concatenate([ss[:, :LANES], ss[:, LANES:]], axis=0) * (1.0 / HEAD_DIM)
    o = o * lax.rsqrt(ms + RMS_EPS) * normo_ref[...]
    z = z_ref[...].astype(F32)
    o_ref[...] = (o * (z * _sigmoid(z))).astype(o_ref.dtype)


def _dn(qkv, z, gates, normo2, b, t):
    n = t // CHUNK
    n_items = b * N_PAIRS

    def cur_spec(base):
        def index(s):
            i = jnp.minimum(s, n_items - 1)
            return (i // N_PAIRS, 0, base + i % N_PAIRS)
        return pl.BlockSpec((None, t, LANES), index)

    def lag_index(s):
        i = jnp.maximum(s - 1, 0)
        return (i // N_PAIRS, 0, i % N_PAIRS)

    return pl.pallas_call(
        functools.partial(_dn_kernel, t=t, n_items=n_items),
        grid=(n_items + 1,),
        in_specs=[
            cur_spec(0), cur_spec(N_PAIRS), cur_spec(2 * N_PAIRS),
            pl.BlockSpec((None, t, LANES), lambda s: (jnp.minimum(s, n_items - 1) // N_PAIRS, 0, 0)),
            pl.BlockSpec((None, t, LANES), lag_index),
            _const_spec((1, LANES)),
        ],
        out_specs=pl.BlockSpec((None, t, LANES), lag_index),
        out_shape=jax.ShapeDtypeStruct((b, t, D_DN), BF16),
        scratch_shapes=[
            pltpu.VMEM((2, 2, n, CHUNK, LANES), BF16),
            pltpu.VMEM((2, 2, n, CHUNK, LANES), BF16),
            pltpu.VMEM((2, 2, t, LANES), BF16),
            pltpu.VMEM((2, 2, t, LANES), F32),
            pltpu.VMEM((2, 2, n, 8, LANES), F32),
            pltpu.VMEM((2, t, LANES), F32),
        ],
        compiler_params=_params(("arbitrary",)),
        name="dn",
    )(qkv, qkv, qkv, gates, z, normo2)


def _mem_kv_kernel(m_ref, g_ref, w_ref, k_ref, v_ref):
    mn = _rms(m_ref[...], g_ref[...]).astype(BF16)
    k_ref[...] = _dot(mn, w_ref[:, :D_MODEL]).astype(k_ref.dtype)
    v_ref[...] = _dot(mn, w_ref[:, D_MODEL:]).astype(v_ref.dtype)


def _mem_kv(mem, g, w_kv):
    b = mem.shape[0]
    blk = pl.BlockSpec((None, N_MEM, D_MODEL), lambda i: (i, 0, 0))
    return pl.pallas_call(
        _mem_kv_kernel,
        grid=(b,),
        in_specs=[blk, _const_spec((1, D_MODEL)), _const_spec((D_MODEL, 2 * D_MODEL))],
        out_specs=[blk, blk],
        out_shape=[jax.ShapeDtypeStruct((b, N_MEM, D_MODEL), BF16)] * 2,
        compiler_params=_params(("parallel",)),
        name="mem_kv",
    )(mem, g, w_kv)


def _mix_xattn_kernel(x_ref, yna_ref, ydn_ref, wout_ref, g_ref, wq_ref, k_ref, v_ref, wo_ref, o_ref):
    x = x_ref[...] + _dot(yna_ref[...], wout_ref[:D_NA, :]) + _dot(ydn_ref[...], wout_ref[D_NA:, :])
    xn = _rms(x, g_ref[...]).astype(BF16)
    q = (_dot(xn, wq_ref[...]) * (HEAD_DIM_X ** -0.5)).astype(BF16)
    head_cols = [slice(h * HEAD_DIM_X, (h + 1) * HEAD_DIM_X) for h in range(N_HEADS_X)]
    scores = [_dot_nt(q[:, cols], k_ref[:, cols]) for cols in head_cols]
    probs = []
    for s in scores:
        e = jnp.exp(s - jnp.max(s, axis=-1, keepdims=True))
        probs.append((e.astype(BF16), jnp.sum(e, axis=-1, keepdims=True)))
    heads = [(_dot(e, v_ref[:, cols]) * (1.0 / l)).astype(BF16) for (e, l), cols in zip(probs, head_cols)]
    o_ref[...] = x + _dot(jnp.concatenate(heads, axis=-1), wo_ref[...])


def _mix_xattn(x, y_na, y_dn, w_out, g, w_q, kmem, vmem, w_o):
    b, t, _ = x.shape
    tm = _token_block(t)
    tok = lambda d: pl.BlockSpec((None, tm, d), lambda i, j: (i, j, 0))
    memspec = pl.BlockSpec((None, N_MEM, D_MODEL), lambda i, j: (i, 0, 0))
    return pl.pallas_call(
        _mix_xattn_kernel,
        grid=(b, t // tm),
        in_specs=[
            tok(D_MODEL), tok(D_NA), tok(D_DN),
            _const_spec((D_NA + D_DN, D_MODEL)),
            _const_spec((1, D_MODEL)),
            _const_spec((D_MODEL, D_MODEL)),
            memspec, memspec,
            _const_spec((D_MODEL, D_MODEL)),
        ],
        out_specs=tok(D_MODEL),
        out_shape=jax.ShapeDtypeStruct((b, t, D_MODEL), F32),
        compiler_params=_params(("parallel", "parallel")),
        name="mix_xattn",
    )(x, y_na, y_dn, w_out, g, w_q, kmem, vmem, w_o)


def _ffn_kernel(x_ref, xp_ref, xnx_ref, g_ref, wu_ref, cw_ref, cb_ref, wd_ref, gf_ref, o_ref, *, tm, final):
    xe = _normed_with_halo(x_ref, xp_ref, xnx_ref, g_ref[...])
    val_cols = lambda c: slice(c * FF_CHUNK, (c + 1) * FF_CHUNK)
    gate_cols = lambda c: slice(D_FF + c * FF_CHUNK, D_FF + (c + 1) * FF_CHUNK)
    up = lambda c: (_dot(xe, wu_ref[:, val_cols(c)]), _dot(xe, wu_ref[:, gate_cols(c)]))
    ups = [up(c) for c in range(FFN_AHEAD)]
    acts = []
    for c in range(N_FF_CHUNKS):
        hv, hg = ups.pop(0)
        if c + FFN_AHEAD < N_FF_CHUNKS:
            ups.append(up(c + FFN_AHEAD))
        val = _token_conv3(hv, cw_ref[:, val_cols(c)], tm) + cb_ref[:, val_cols(c)]
        gate = _token_conv3(hg, cw_ref[:, gate_cols(c)], tm) + cb_ref[:, gate_cols(c)]
        acts.append((gate * _sigmoid(gate) * val).astype(BF16))
    acc = x_ref[...] + _dot(jnp.concatenate(acts, axis=1), wd_ref[...])
    if final:
        acc = _rms(acc, gf_ref[...])
    o_ref[...] = acc


def _ffn(x, g, w_up, conv_w, conv_b, w_down, g_final, final):
    b, t, _ = x.shape
    tm = _token_block(t)
    return pl.pallas_call(
        functools.partial(_ffn_kernel, tm=tm, final=final),
        grid=(b, t // tm),
        in_specs=_halo_specs(tm, t, D_MODEL) + [
            _const_spec((1, D_MODEL)),
            _const_spec((D_MODEL, 2 * D_FF)),
            _const_spec((3, 2 * D_FF)),
            _const_spec((1, 2 * D_FF)),
            _const_spec((D_FF, D_MODEL)),
            _const_spec((1, D_MODEL)),
        ],
        out_specs=pl.BlockSpec((None, tm, D_MODEL), lambda i, j: (i, j, 0)),
        out_shape=jax.ShapeDtypeStruct((b, t, D_MODEL), F32),
        compiler_params=_params(("parallel", "parallel")),
        name="ffn",
    )(x, x, x, g, w_up, conv_w, conv_b, w_down, g_final)


def _prep_layer(l, norm_mix, w_in, rpb, conv_qkv, a_log, dt_bias, norm_o, w_out, norm_x, norm_mem, w_xq, w_xkv, w_xo,
                norm_ffn, w_up, conv_ffn, conv_ffn_b, w_down):
    row = lambda v: v.reshape(1, -1).astype(F32)
    w_gate = jnp.pad(w_in[l][:, D_MAIN:], ((0, 0), (0, LANES - N_GATE))).astype(BF16)
    pad = jnp.zeros((2 * N_HEADS_DN,), F32)
    tail = jnp.zeros((LANES - N_GATE,), F32)
    prm = jnp.stack([jnp.concatenate([pad, a_log[l].reshape(-1), tail]),
                     jnp.concatenate([pad, dt_bias[l].reshape(-1), tail])])
    return dict(
        norm_mix=row(norm_mix[l]), w_main=w_in[l][:, :D_MAIN].astype(BF16), w_gate=w_gate,
        na_bias=_na_bias_tables(rpb[l]), conv_qkv=conv_qkv[l], prm=prm,
        norm_o=row(jnp.concatenate([norm_o[l], norm_o[l]])), w_out=w_out[l].astype(BF16),
        norm_x=row(norm_x[l]), norm_mem=row(norm_mem[l]), w_xq=w_xq[l].astype(BF16), w_xkv=w_xkv[l].astype(BF16),
        w_xo=w_xo[l].astype(BF16), norm_ffn=row(norm_ffn[l]), w_up=w_up[l].astype(BF16), conv_ffn=conv_ffn[l],
        conv_ffn_b=row(conv_ffn_b[l]), w_down=w_down[l].astype(BF16))


def _layer(x, mem, p, g_final, final):
    b, t, _ = x.shape
    qkv_na, qkv_dn, z, gates = _in_proj(x, p["norm_mix"], p["w_main"], p["w_gate"], p["conv_qkv"], p["prm"])
    y_na = _na(qkv_na, p["na_bias"], b, t)
    y_dn = _dn(qkv_dn, z, gates, p["norm_o"], b, t)
    kmem, vmem = _mem_kv(mem, p["norm_mem"], p["w_xkv"])
    x = _mix_xattn(x, y_na, y_dn, p["w_out"], p["norm_x"], p["w_xq"], kmem, vmem, p["w_xo"])
    return _ffn(x, p["norm_ffn"], p["w_up"], p["conv_ffn"], p["conv_ffn_b"], p["w_down"], g_final, final)


def kernel(x_prompt, x_sample, mem_prompt, mem_sample, norm_mix, w_in, rpb, conv_qkv, a_log, dt_bias, norm_o, w_out,
           norm_x, norm_mem, w_xq, w_xkv, w_xo, norm_ffn, w_up, conv_ffn, conv_ffn_b, w_down, norm_final):
    layers = [_prep_layer(l, norm_mix, w_in, rpb, conv_qkv, a_log, dt_bias, norm_o, w_out, norm_x, norm_mem, w_xq,
                          w_xkv, w_xo, norm_ffn, w_up, conv_ffn, conv_ffn_b, w_down) for l in range(DEPTH)]
    g_final = norm_final.reshape(1, -1).astype(F32)
    outs = []
    for x, mem in ((x_prompt, mem_prompt), (x_sample, mem_sample)):
        for l in range(DEPTH):
            x = _layer(x, mem, layers[l], g_final, l == DEPTH - 1)
        outs.append(x)
    return tuple(outs)
```

```python
import functools

import numpy as np
import jax
import jax.numpy as jnp
from jax import lax
from jax.experimental import pallas as pl
from jax.experimental.pallas import tpu as pltpu

F32 = jnp.float32
BF16 = jnp.bfloat16

D_MODEL = 1024
DEPTH = 4
HEAD_DIM = 64
N_HEADS_NA = 8
N_HEADS_DN = 8
D_NA = N_HEADS_NA * HEAD_DIM
D_DN = N_HEADS_DN * HEAD_DIM
GRID_W = 64
WIN_H = 8
WIN_W = 16
CHUNK = 64
N_MEM = 256
N_HEADS_X = 4
HEAD_DIM_X = D_MODEL // N_HEADS_X
D_FF = 2816
RMS_EPS = 1e-6
L2_EPS = 1e-6

LANES = 128
MXU_DIM = 256
N_PAIRS = N_HEADS_DN // 2
D_MAIN = 3 * D_NA + 4 * D_DN
N_GATE = 4 * N_HEADS_DN
FF_CHUNK = 256
N_FF_CHUNKS = D_FF // FF_CHUNK
FFN_AHEAD = 2
HALO = 8
NA_ROWS = 8
NA_AHEAD = 4
DN_GROUP = 8
VMEM_LIMIT = 56 * 1024 * 1024


def _token_block(t):
    return min(1024, t)


def _rms(x, g):
    return x * lax.rsqrt(jnp.mean(x * x, axis=-1, keepdims=True) + RMS_EPS) * g


def _sigmoid(x):
    return 1.0 / (1.0 + jnp.exp(-x))


def _split2(x):
    hi = x.astype(BF16)
    lo = (x - hi.astype(F32)).astype(BF16)
    return hi, lo


def _dot(a, b):
    return jnp.dot(a, b, preferred_element_type=F32)


def _dot_nt(a, b):
    return lax.dot_general(a, b, (((1,), (1,)), ((), ())), preferred_element_type=F32)


def _head_block_ones(n):
    r = lax.broadcasted_iota(jnp.int32, (n, n), 0)
    c = lax.broadcasted_iota(jnp.int32, (n, n), 1)
    return ((r // HEAD_DIM) == (c // HEAD_DIM)).astype(BF16)


def _const_spec(shape):
    nd = len(shape)
    return pl.BlockSpec(shape, lambda *_: (0,) * nd)


def _params(sem):
    return pltpu.CompilerParams(dimension_semantics=sem, vmem_limit_bytes=VMEM_LIMIT)


def _halo_specs(tm, t, d):
    nh = tm // HALO
    last = t // HALO - 1
    return [
        pl.BlockSpec((None, tm, d), lambda i, j: (i, j, 0)),
        pl.BlockSpec((None, HALO, d), lambda i, j: (i, jnp.maximum(j * nh - 1, 0), 0)),
        pl.BlockSpec((None, HALO, d), lambda i, j: (i, jnp.minimum((j + 1) * nh, last), 0)),
    ]


def _normed_with_halo(x_ref, xp_ref, xnx_ref, g):
    j = pl.program_id(1)
    has_prev = jnp.where(j > 0, 1.0, 0.0)
    has_next = jnp.where(j < pl.num_programs(1) - 1, 1.0, 0.0)
    return jnp.concatenate([_rms(xp_ref[...], g) * has_prev, _rms(x_ref[...], g), _rms(xnx_ref[...], g) * has_next],
                           axis=0).astype(BF16)


def _token_conv3(h, w, tm):
    ext = tm + 2 * HALO
    hp = pltpu.roll(h, 1, 0)[HALO:HALO + tm]
    hn = pltpu.roll(h, ext - 1, 0)[HALO:HALO + tm]
    return hp * w[0:1] + h[HALO:HALO + tm] * w[1:2] + hn * w[2:3]


def _in_proj_kernel(x_ref, xp_ref, xnx_ref, g_ref, wm_ref, wg_ref, cw_ref, prm_ref, na_ref, dn_ref, z_ref, gate_ref,
                    *, tm):
    xe = _normed_with_halo(x_ref, xp_ref, xnx_ref, g_ref[...])
    xm = xe[HALO:HALO + tm]

    def na_part(j):
        r = _dot(xm, wm_ref[:, j * D_NA:(j + 1) * D_NA])
        if j == 0:
            r = r * (HEAD_DIM ** -0.5)
        na_ref[:, j * D_NA:(j + 1) * D_NA] = r.astype(na_ref.dtype)

    def z_part():
        off = 3 * D_NA + 3 * D_DN
        z_ref[...] = _dot(xm, wm_ref[:, off:off + D_DN]).astype(z_ref.dtype)

    def gate_part():
        raw = _dot(xm, wg_ref[...])
        col = lax.broadcasted_iota(jnp.int32, raw.shape, 1)
        xx = raw + prm_ref[1:2]
        softplus = jnp.maximum(xx, 0.0) + jnp.log(1.0 + jnp.exp(-jnp.abs(xx)))
        gate_ref[...] = jnp.where(col < 2 * N_HEADS_DN, _sigmoid(raw), -jnp.exp(prm_ref[0:1]) * softplus)

    fillers = [functools.partial(na_part, 0), functools.partial(na_part, 1), functools.partial(na_part, 2),
               z_part, gate_part]

    ones_bd = _head_block_ones(MXU_DIM)
    n_blocks = 3 * D_DN // MXU_DIM

    def dn_proj(c):
        off = 3 * D_NA + c * MXU_DIM
        return _dot(xe, wm_ref[:, off:off + MXU_DIM])

    h_next = dn_proj(0)
    for c in range(n_blocks):
        h = h_next
        if c + 1 < n_blocks:
            h_next = dn_proj(c + 1)
        if fillers:
            fillers.pop(0)()
        cols = slice(c * MXU_DIM, (c + 1) * MXU_DIM)
        y = _token_conv3(h, cw_ref[:, cols], tm)
        y = y * _sigmoid(y)
        if c < 2 * D_DN // MXU_DIM:
            y = y * lax.rsqrt(_dot((y * y).astype(BF16), ones_bd) + L2_EPS)
        if c < D_DN // MXU_DIM:
            y = y * (HEAD_DIM ** -0.5)
        dn_ref[:, cols] = y.astype(dn_ref.dtype)
    for f in fillers:
        f()


def _in_proj(x, g, w_main, w_gate, conv_w, prm):
    b, t, _ = x.shape
    tm = _token_block(t)
    tok = lambda d: pl.BlockSpec((None, tm, d), lambda i, j: (i, j, 0))
    return pl.pallas_call(
        functools.partial(_in_proj_kernel, tm=tm),
        grid=(b, t // tm),
        in_specs=_halo_specs(tm, t, D_MODEL) + [
            _const_spec((1, D_MODEL)),
            _const_spec((D_MODEL, D_MAIN)),
            _const_spec((D_MODEL, LANES)),
            _const_spec((3, 3 * D_DN)),
            _const_spec((2, LANES)),
        ],
        out_specs=[tok(3 * D_NA), tok(3 * D_DN), tok(D_DN), tok(LANES)],
        out_shape=[
            jax.ShapeDtypeStruct((b, t, 3 * D_NA), BF16),
            jax.ShapeDtypeStruct((b, t, 3 * D_DN), BF16),
            jax.ShapeDtypeStruct((b, t, D_DN), BF16),
            jax.ShapeDtypeStruct((b, t, LANES), F32),
        ],
        compiler_params=_params(("parallel", "parallel")),
        name="in_proj",
    )(x, x, x, g, w_main, w_gate, conv_w, prm)


def _na_bias_tables(rpb_l):
    c = np.arange(GRID_W)
    cs = np.clip(c - WIN_W // 2, 0, GRID_W - WIN_W)
    valid = (c[None, :] >= cs[:, None]) & (c[None, :] < cs[:, None] + WIN_W)
    coff = np.clip(c[None, :] - c[:, None] + (WIN_W - 1), 0, 2 * WIN_W - 2)
    roff = np.arange(WIN_H)[None, :] + (WIN_H - 1) - np.arange(WIN_H)[:, None]
    select = np.asarray(coff[..., None] == np.arange(2 * WIN_W - 1), np.float32)
    b = jnp.einsum("vhib,qkb->vhqik", jnp.transpose(rpb_l[:, roff], (1, 0, 2, 3)), select,
                   precision=lax.Precision.HIGHEST)
    b = jnp.where(valid[None, None, :, None, :], b, -jnp.inf)
    return b.reshape(WIN_H, N_PAIRS, 2 * GRID_W, WIN_H * GRID_W).astype(F32)


def _na_kernel(q_ref, k_ref, v_ref, bias_ref, o_ref, *, rows):
    step = pl.program_id(1)
    nk = WIN_H * GRID_W
    lane = lax.broadcasted_iota(jnp.int32, (GRID_W, LANES), 1)
    low = lane < HEAD_DIM

    problems = []
    for rr in range(NA_ROWS):
        r = step * NA_ROWS + rr
        rs = jnp.clip(r - WIN_H // 2, 0, rows - WIN_H)
        start = pl.multiple_of(rs * GRID_W, GRID_W)
        for p in range(N_PAIRS):
            problems.append((rr, p, r - rs, start))

    def scores(rr, p, var, start):
        cols = slice(p * LANES, (p + 1) * LANES)
        q2 = q_ref[rr * GRID_W:(rr + 1) * GRID_W, cols].astype(F32)
        qs = jnp.concatenate([jnp.where(low, q2, 0.0), jnp.where(low, 0.0, q2)], axis=0).astype(BF16)
        return _dot_nt(qs, k_ref[pl.ds(start, nk), cols]) + bias_ref[var, p]

    pending = [scores(*prob) for prob in problems[:NA_AHEAD]]
    for i, (rr, p, var, start) in enumerate(problems):
        s = pending.pop(0)
        if i + NA_AHEAD < len(problems):
            pending.append(scores(*problems[i + NA_AHEAD]))
        e = jnp.exp(s - jnp.max(s, axis=-1, keepdims=True))
        l = jnp.sum(e, axis=-1, keepdims=True)
        cols = slice(p * LANES, (p + 1) * LANES)
        o = _dot(e.astype(BF16), v_ref[pl.ds(start, nk), cols]) * (1.0 / l)
        o_ref[rr * GRID_W:(rr + 1) * GRID_W, cols] = jnp.where(low, o[:GRID_W], o[GRID_W:]).astype(o_ref.dtype)


def _na(qkv, bias, b, t):
    rows = t // GRID_W
    blk = NA_ROWS * GRID_W
    return pl.pallas_call(
        functools.partial(_na_kernel, rows=rows),
        grid=(b, rows // NA_ROWS),
        in_specs=[
            pl.BlockSpec((None, blk, D_NA), lambda i, r: (i, r, 0)),
            pl.BlockSpec((None, t, D_NA), lambda i, r: (i, 0, 1)),
            pl.BlockSpec((None, t, D_NA), lambda i, r: (i, 0, 2)),
            _const_spec((WIN_H, N_PAIRS, 2 * GRID_W, WIN_H * GRID_W)),
        ],
        out_specs=pl.BlockSpec((None, blk, D_NA), lambda i, r: (i, r, 0)),
        out_shape=jax.ShapeDtypeStruct((b, t, D_NA), BF16),
        compiler_params=_params(("parallel", "arbitrary")),
        name="na",
    )(qkv, qkv, qkv, bias)


def _block_diag(x, low):
    return jnp.concatenate([jnp.where(low, x, 0.0), jnp.where(low, 0.0, x)], axis=0)


def _mm(a, b):
    return _dot(a.astype(BF16), b.astype(BF16))


def _neumann(nmats, low, eye2, between):
    ps = [_mm(nm, _block_diag(nm, low)) for nm in nmats]
    ts = [eye2 + nm for nm in nmats]
    between()
    for _ in range(4):
        outs = [_mm(p, jnp.concatenate([_block_diag(p, low), _block_diag(tt, low)], axis=1)) for p, tt in zip(ps, ts)]
        ps = [o[:, :LANES] for o in outs]
        ts = [tt + o[:, LANES:] for tt, o in zip(ts, outs)]
        between()
    res = [tt + _mm(p, _block_diag(tt, low)) for p, tt in zip(ps, ts)]
    between()
    return res


def _dn_kernel(q_ref, k_ref, v_ref, gate_ref, z_ref, normo_ref, o_ref, m_s, n_s, p_s, r_s, gl_s, out_s,
               *, t, n_items):
    n = t // CHUNK
    group = min(DN_GROUP, n)
    grp_rows = group * CHUNK
    step = pl.program_id(0)
    cur = step % 2
    prev = 1 - cur
    pair = jnp.minimum(step, n_items - 1) % N_PAIRS

    @pl.when(step == 0)
    def _():
        m_s[1] = jnp.zeros(m_s.shape[1:], m_s.dtype)
        n_s[1] = jnp.zeros(n_s.shape[1:], n_s.dtype)
        p_s[1] = jnp.zeros(p_s.shape[1:], p_s.dtype)
        r_s[1] = jnp.zeros(r_s.shape[1:], r_s.dtype)
        gl_s[1] = jnp.zeros(gl_s.shape[1:], gl_s.dtype)

    er = lax.broadcasted_iota(jnp.int32, (LANES, 4 * LANES), 0)
    ec = lax.broadcasted_iota(jnp.int32, (LANES, 4 * LANES), 1)
    src = (ec // LANES) * N_HEADS_DN + 2 * pair + (ec % LANES) // HEAD_DIM
    spread = (er == src).astype(BF16)
    spread2 = jnp.concatenate([spread, spread], axis=0)

    lane = lax.broadcasted_iota(jnp.int32, (CHUNK, LANES), 1)
    low = lane < HEAD_DIM
    ri = lax.broadcasted_iota(jnp.int32, (CHUNK, LANES), 0)
    ci = lane % HEAD_DIM
    eye2 = (ri == ci).astype(F32)
    ti = lax.broadcasted_iota(jnp.int32, (CHUNK, CHUNK), 0)
    tj = lax.broadcasted_iota(jnp.int32, (CHUNK, CHUNK), 1)
    neg_ones = jnp.full((CHUNK, CHUNK), -1.0, F32)

    dirs = []
    for tri, keep, upto, strict, last in (
            ((tj <= ti), ci <= ri, ci >= ri, ci < ri, CHUNK - 1),
            ((tj >= ti), ci >= ri, ci <= ri, ci > ri, 0)):
        trif = tri.astype(F32)
        lhs = jnp.concatenate([trif, neg_ones, trif, neg_ones], axis=1).astype(BF16)
        dirs.append((lhs, keep, upto, strict, last))

    def body(grp, states):
        states = list(states)
        done = [0]

        def recurrence_steps(count):
            for _ in range(count):
                k = grp * group + done[0]
                done[0] += 1
                for d, c in enumerate((k, n - 1 - k)):
                    rows = pl.ds(pl.multiple_of(c * CHUNK, CHUNK), CHUNK)
                    s2 = states[d]
                    lhs = jnp.concatenate([m_s[prev, d, c], p_s[prev, d, rows, :]], axis=0)
                    ms_ps = _dot(lhs, _block_diag(s2, low).astype(BF16))
                    out_s[d, rows, :] = ms_ps[CHUNK:] + r_s[prev, d, rows, :]
                    states[d] = s2 * gl_s[prev, d, c][0:1, :] + n_s[prev, d, c].astype(F32) - ms_ps[:CHUNK]

        rounds = 10
        schedule = iter([group // rounds + (1 if i < group % rounds else 0) for i in range(rounds)])
        between = lambda: recurrence_steps(next(schedule))

        r0 = pl.multiple_of(grp * grp_rows, grp_rows)
        rows_g = pl.ds(r0, grp_rows)
        kf = k_ref[rows_g, :].astype(F32)
        qf = q_ref[rows_g, :].astype(F32)
        vf = v_ref[rows_g, :].astype(F32)
        ghi, glo = _split2(gate_ref[rows_g, :])
        sp = _dot(jnp.concatenate([ghi, glo], axis=1), spread2)

        chunks = []
        for u in range(group):
            sl = slice(u * CHUNK, (u + 1) * CHUNK)
            kc, qc, vc = kf[sl], qf[sl], vf[sl]
            kq = jnp.concatenate([kc, qc], axis=0).astype(BF16)
            gq = _dot_nt(kq, _block_diag(kc, low).astype(BF16))
            chunks.append((kc, qc, vc, gq[:CHUNK], gq[CHUNK:], sp[sl]))
        between()

        dgs = []
        for kc, qc, vc, gram, qk, spc in chunks:
            for d, (lhs, keep, upto, strict, last) in enumerate(dirs):
                ghi2, glo2 = _split2(spc[:, (2 + d) * LANES:(3 + d) * LANES])
                zero = jnp.zeros_like(ghi2)
                rhs = jnp.concatenate([
                    jnp.concatenate([ghi2, jnp.where(upto, ghi2, zero), glo2, jnp.where(upto, glo2, zero)], axis=0),
                    jnp.concatenate([ghi2, zero, glo2, zero], axis=0)], axis=1)
                dgs.append(_dot(lhs, rhs))
        between()

        probs = []
        nmats = []
        it = iter(dgs)
        for kc, qc, vc, gram, qk, spc in chunks:
            for d, (lhs, keep, upto, strict, last) in enumerate(dirs):
                dg = next(it)
                beta = spc[:, d * LANES:(d + 1) * LANES]
                delta = dg[:, :LANES]
                gcol = dg[:, LANES:]
                decay = jnp.where(keep, jnp.exp(jnp.minimum(delta, 0.0)), 0.0)
                gamma = jnp.exp(gcol)
                glast = gcol[last:last + 1, :]
                nmats.append(jnp.where(strict, -(beta * gram * decay), 0.0))
                kd = kc * jnp.exp(glast - gcol)
                kd_t = jnp.transpose(jnp.concatenate([kd, jnp.zeros_like(kd)], axis=0))
                kdt = kd_t[:HEAD_DIM] + pltpu.roll(kd_t[HEAD_DIM:], HEAD_DIM, 1)
                probs.append((kc, qc, vc, beta, gamma, glast, qk * decay, kdt))
        tinvs = _neumann(nmats, low, eye2, between)

        uws = []
        for (kc, qc, vc, beta, gamma, glast, qkd, kdt), tinv in zip(probs, tinvs):
            rhs = jnp.concatenate([_block_diag(kc * (beta * gamma), low), _block_diag(vc * beta, low)], axis=1)
            uws.append(_mm(tinv, rhs))
        between()

        mnprs = []
        for (kc, qc, vc, beta, gamma, glast, qkd, kdt), wu in zip(probs, uws):
            lhs = jnp.concatenate([kdt, qkd], axis=0)
            rhs = jnp.concatenate([_block_diag(wu[:, :LANES], low), _block_diag(wu[:, LANES:], low)], axis=1)
            mnprs.append(_mm(lhs, rhs))
        between()

        idx = 0
        for u in range(group):
            c = grp * group + u
            rows = pl.ds(pl.multiple_of(r0 + u * CHUNK, CHUNK), CHUNK)
            for d in range(2):
                kc, qc, vc, beta, gamma, glast, qkd, kdt = probs[idx]
                mn, pr = mnprs[idx][:CHUNK], mnprs[idx][CHUNK:]
                m_s[cur, d, c] = mn[:, :LANES].astype(BF16)
                n_s[cur, d, c] = mn[:, LANES:].astype(BF16)
                p_s[cur, d, rows, :] = (qc * gamma - pr[:, :LANES]).astype(BF16)
                r_s[cur, d, rows, :] = pr[:, LANES:]
                gl_s[cur, d, c] = jnp.broadcast_to(jnp.exp(glast), (8, LANES))
                idx += 1
        assert done[0] == group
        return tuple(states)

    zero_state = jnp.zeros((CHUNK, LANES), F32)
    lax.fori_loop(0, n // group, body, (zero_state, zero_state), unroll=4)

    o = out_s[0] + out_s[1]
    sq = (o * o).astype(BF16)
    ss = _dot(jnp.concatenate([sq[:t // 2], sq[t // 2:]], axis=1), _head_block_ones(MXU_DIM))
    ms = jnp.concatenate([ss[:, :LANES], ss[:, LANES:]], axis=0) * (1.0 / HEAD_DIM)
    o = o * lax.rsqrt(ms + RMS_EPS) * normo_ref[...]
    z = z_ref[...].astype(F32)
    o_ref[...] = (o * (z * _sigmoid(z))).astype(o_ref.dtype)


def _dn(qkv, z, gates, normo2, b, t):
    n = t // CHUNK
    n_items = b * N_PAIRS

    def cur_spec(base):
        def index(s):
            i = jnp.minimum(s, n_items - 1)
            return (i // N_PAIRS, 0, base + i % N_PAIRS)
        return pl.BlockSpec((None, t, LANES), index)

    def lag_index(s):
        i = jnp.maximum(s - 1, 0)
        return (i // N_PAIRS, 0, i % N_PAIRS)

    return pl.pallas_call(
        functools.partial(_dn_kernel, t=t, n_items=n_items),
        grid=(n_items + 1,),
        in_specs=[
            cur_spec(0), cur_spec(N_PAIRS), cur_spec(2 * N_PAIRS),
            pl.BlockSpec((None, t, LANES), lambda s: (jnp.minimum(s, n_items - 1) // N_PAIRS, 0, 0)),
            pl.BlockSpec((None, t, LANES), lag_index),
            _const_spec((1, LANES)),
        ],
        out_specs=pl.BlockSpec((None, t, LANES), lag_index),
        out_shape=jax.ShapeDtypeStruct((b, t, D_DN), BF16),
        scratch_shapes=[
            pltpu.VMEM((2, 2, n, CHUNK, LANES), BF16),
            pltpu.VMEM((2, 2, n, CHUNK, LANES), BF16),
            pltpu.VMEM((2, 2, t, LANES), BF16),
            pltpu.VMEM((2, 2, t, LANES), F32),
            pltpu.VMEM((2, 2, n, 8, LANES), F32),
            pltpu.VMEM((2, t, LANES), F32),
        ],
        compiler_params=_params(("arbitrary",)),
        name="dn",
    )(qkv, qkv, qkv, gates, z, normo2)


def _mem_kv_kernel(m_ref, g_ref, w_ref, k_ref, v_ref):
    mn = _rms(m_ref[...], g_ref[...]).astype(BF16)
    k_ref[...] = _dot(mn, w_ref[:, :D_MODEL]).astype(k_ref.dtype)
    v_ref[...] = _dot(mn, w_ref[:, D_MODEL:]).astype(v_ref.dtype)


def _mem_kv(mem, g, w_kv):
    b = mem.shape[0]
    blk = pl.BlockSpec((None, N_MEM, D_MODEL), lambda i: (i, 0, 0))
    return pl.pallas_call(
        _mem_kv_kernel,
        grid=(b,),
        in_specs=[blk, _const_spec((1, D_MODEL)), _const_spec((D_MODEL, 2 * D_MODEL))],
        out_specs=[blk, blk],
        out_shape=[jax.ShapeDtypeStruct((b, N_MEM, D_MODEL), BF16)] * 2,
        compiler_params=_params(("parallel",)),
        name="mem_kv",
    )(mem, g, w_kv)


def _mix_xattn_kernel(x_ref, yna_ref, ydn_ref, wout_ref, g_ref, wq_ref, k_ref, v_ref, wo_ref, o_ref):
    x = x_ref[...] + _dot(yna_ref[...], wout_ref[:D_NA, :]) + _dot(ydn_ref[...], wout_ref[D_NA:, :])
    xn = _rms(x, g_ref[...]).astype(BF16)
    q = (_dot(xn, wq_ref[...]) * (HEAD_DIM_X ** -0.5)).astype(BF16)
    head_cols = [slice(h * HEAD_DIM_X, (h + 1) * HEAD_DIM_X) for h in range(N_HEADS_X)]
    scores = [_dot_nt(q[:, cols], k_ref[:, cols]) for cols in head_cols]
    probs = []
    for s in scores:
        e = jnp.exp(s - jnp.max(s, axis=-1, keepdims=True))
        probs.append((e.astype(BF16), jnp.sum(e, axis=-1, keepdims=True)))
    heads = [(_dot(e, v_ref[:, cols]) * (1.0 / l)).astype(BF16) for (e, l), cols in zip(probs, head_cols)]
    o_ref[...] = x + _dot(jnp.concatenate(heads, axis=-1), wo_ref[...])


def _mix_xattn(x, y_na, y_dn, w_out, g, w_q, kmem, vmem, w_o):
    b, t, _ = x.shape
    tm = _token_block(t)
    tok = lambda d: pl.BlockSpec((None, tm, d), lambda i, j: (i, j, 0))
    memspec = pl.BlockSpec((None, N_MEM, D_MODEL), lambda i, j: (i, 0, 0))
    return pl.pallas_call(
        _mix_xattn_kernel,
        grid=(b, t // tm),
        in_specs=[
            tok(D_MODEL), tok(D_NA), tok(D_DN),
            _const_spec((D_NA + D_DN, D_MODEL)),
            _const_spec((1, D_MODEL)),
            _const_spec((D_MODEL, D_MODEL)),
            memspec, memspec,
            _const_spec((D_MODEL, D_MODEL)),
        ],
        out_specs=tok(D_MODEL),
        out_shape=jax.ShapeDtypeStruct((b, t, D_MODEL), F32),
        compiler_params=_params(("parallel", "parallel")),
        name="mix_xattn",
    )(x, y_na, y_dn, w_out, g, w_q, kmem, vmem, w_o)


def _ffn_kernel(x_ref, xp_ref, xnx_ref, g_ref, wu_ref, cw_ref, cb_ref, wd_ref, gf_ref, o_ref, *, tm, final):
    xe = _normed_with_halo(x_ref, xp_ref, xnx_ref, g_ref[...])
    val_cols = lambda c: slice(c * FF_CHUNK, (c + 1) * FF_CHUNK)
    gate_cols = lambda c: slice(D_FF + c * FF_CHUNK, D_FF + (c + 1) * FF_CHUNK)
    up = lambda c: (_dot(xe, wu_ref[:, val_cols(c)]), _dot(xe, wu_ref[:, gate_cols(c)]))
    ups = [up(c) for c in range(FFN_AHEAD)]
    acts = []
    for c in range(N_FF_CHUNKS):
        hv, hg = ups.pop(0)
        if c + FFN_AHEAD < N_FF_CHUNKS:
            ups.append(up(c + FFN_AHEAD))
        val = _token_conv3(hv, cw_ref[:, val_cols(c)], tm) + cb_ref[:, val_cols(c)]
        gate = _token_conv3(hg, cw_ref[:, gate_cols(c)], tm) + cb_ref[:, gate_cols(c)]
        acts.append((gate * _sigmoid(gate) * val).astype(BF16))
    acc = x_ref[...] + _dot(jnp.concatenate(acts, axis=1), wd_ref[...])
    if final:
        acc = _rms(acc, gf_ref[...])
    o_ref[...] = acc


def _ffn(x, g, w_up, conv_w, conv_b, w_down, g_final, final):
    b, t, _ = x.shape
    tm = _token_block(t)
    return pl.pallas_call(
        functools.partial(_ffn_kernel, tm=tm, final=final),
        grid=(b, t // tm),
        in_specs=_halo_specs(tm, t, D_MODEL) + [
            _const_spec((1, D_MODEL)),
            _const_spec((D_MODEL, 2 * D_FF)),
            _const_spec((3, 2 * D_FF)),
            _const_spec((1, 2 * D_FF)),
            _const_spec((D_FF, D_MODEL)),
            _const_spec((1, D_MODEL)),
        ],
        out_specs=pl.BlockSpec((None, tm, D_MODEL), lambda i, j: (i, j, 0)),
        out_shape=jax.ShapeDtypeStruct((b, t, D_MODEL), F32),
        compiler_params=_params(("parallel", "parallel")),
        name="ffn",
    )(x, x, x, g, w_up, conv_w, conv_b, w_down, g_final)


def _prep_layer(l, norm_mix, w_in, rpb, conv_qkv, a_log, dt_bias, norm_o, w_out, norm_x, norm_mem, w_xq, w_xkv, w_xo,
                norm_ffn, w_up, conv_ffn, conv_ffn_b, w_down):
    row = lambda v: v.reshape(1, -1).astype(F32)
    w_gate = jnp.pad(w_in[l][:, D_MAIN:], ((0, 0), (0, LANES - N_GATE))).astype(BF16)
    pad = jnp.zeros((2 * N_HEADS_DN,), F32)
    tail = jnp.zeros((LANES - N_GATE,), F32)
    prm = jnp.stack([jnp.concatenate([pad, a_log[l].reshape(-1), tail]),
                     jnp.concatenate([pad, dt_bias[l].reshape(-1), tail])])
    return dict(
        norm_mix=row(norm_mix[l]), w_main=w_in[l][:, :D_MAIN].astype(BF16), w_gate=w_gate,
        na_bias=_na_bias_tables(rpb[l]), conv_qkv=conv_qkv[l], prm=prm,
        norm_o=row(jnp.concatenate([norm_o[l], norm_o[l]])), w_out=w_out[l].astype(BF16),
        norm_x=row(norm_x[l]), norm_mem=row(norm_mem[l]), w_xq=w_xq[l].astype(BF16), w_xkv=w_xkv[l].astype(BF16),
        w_xo=w_xo[l].astype(BF16), norm_ffn=row(norm_ffn[l]), w_up=w_up[l].astype(BF16), conv_ffn=conv_ffn[l],
        conv_ffn_b=row(conv_ffn_b[l]), w_down=w_down[l].astype(BF16))


def _layer(x, mem, p, g_final, final):
    b, t, _ = x.shape
    qkv_na, qkv_dn, z, gates = _in_proj(x, p["norm_mix"], p["w_main"], p["w_gate"], p["conv_qkv"], p["prm"])
    y_na = _na(qkv_na, p["na_bias"], b, t)
    y_dn = _dn(qkv_dn, z, gates, p["norm_o"], b, t)
    kmem, vmem = _mem_kv(mem, p["norm_mem"], p["w_xkv"])
    x = _mix_xattn(x, y_na, y_dn, p["w_out"], p["norm_x"], p["w_xq"], kmem, vmem, p["w_xo"])
    return _ffn(x, p["norm_ffn"], p["w_up"], p["conv_ffn"], p["conv_ffn_b"], p["w_down"], g_final, final)


def kernel(x_prompt, x_sample, mem_prompt, mem_sample, norm_mix, w_in, rpb, conv_qkv, a_log, dt_bias, norm_o, w_out,
           norm_x, norm_mem, w_xq, w_xkv, w_xo, norm_ffn, w_up, conv_ffn, conv_ffn_b, w_down, norm_final):
    layers = [_prep_layer(l, norm_mix, w_in, rpb, conv_qkv, a_log, dt_bias, norm_o, w_out, norm_x, norm_mem, w_xq,
                          w_xkv, w_xo, norm_ffn, w_up, conv_ffn, conv_ffn_b, w_down) for l in range(DEPTH)]
    g_final = norm_final.reshape(1, -1).astype(F32)
    outs = []
    for x, mem in ((x_prompt, mem_prompt), (x_sample, mem_sample)):
        for l in range(DEPTH):
            x = _layer(x, mem, layers[l], g_final, l == DEPTH - 1)
        outs.append(x)
    return tuple(outs)
```

```python
import functools

import numpy as np
import jax
import jax.numpy as jnp
from jax import lax
from jax.experimental import pallas as pl
from jax.experimental.pallas import tpu as pltpu

F32 = jnp.float32
BF16 = jnp.bfloat16

D_MODEL = 1024
DEPTH = 4
HEAD_DIM = 64
N_HEADS_NA = 8
N_HEADS_DN = 8
D_NA = N_HEADS_NA * HEAD_DIM
D_DN = N_HEADS_DN * HEAD_DIM
GRID_W = 64
WIN_H = 8
WIN_W = 16
CHUNK = 64
N_MEM = 256
N_HEADS_X = 4
HEAD_DIM_X = D_MODEL // N_HEADS_X
D_FF = 2816
RMS_EPS = 1e-6
L2_EPS = 1e-6

LANES = 128
MXU_DIM = 256
N_PAIRS = N_HEADS_DN // 2
D_MAIN = 3 * D_NA + 4 * D_DN
N_GATE = 4 * N_HEADS_DN
FF_CHUNK = 256
N_FF_CHUNKS = D_FF // FF_CHUNK
FFN_AHEAD = 2
HALO = 8
TOKEN_BLOCK = 1024
NA_ROWS = 8
NA_AHEAD = 4
DN_GROUP = 8
DN_UNROLL = 4
VMEM_LIMIT = 56 * 1024 * 1024


def _token_block(t):
    return min(TOKEN_BLOCK, t)


def _rms(x, g):
    return x * lax.rsqrt(jnp.mean(x * x, axis=-1, keepdims=True) + RMS_EPS) * g


def _sigmoid(x):
    return 1.0 / (1.0 + jnp.exp(-x))


def _split2(x):
    hi = x.astype(BF16)
    lo = (x - hi.astype(F32)).astype(BF16)
    return hi, lo


def _dot(a, b):
    return jnp.dot(a, b, preferred_element_type=F32)


def _dot_nt(a, b):
    return lax.dot_general(a, b, (((1,), (1,)), ((), ())), preferred_element_type=F32)


def _head_block_ones(n):
    r = lax.broadcasted_iota(jnp.int32, (n, n), 0)
    c = lax.broadcasted_iota(jnp.int32, (n, n), 1)
    return ((r // HEAD_DIM) == (c // HEAD_DIM)).astype(BF16)


def _const_spec(shape):
    nd = len(shape)
    return pl.BlockSpec(shape, lambda *_: (0,) * nd)


def _params(sem):
    return pltpu.CompilerParams(dimension_semantics=sem, vmem_limit_bytes=VMEM_LIMIT)


def _halo_specs(tm, t, d):
    nh = tm // HALO
    last = t // HALO - 1
    return [
        pl.BlockSpec((None, tm, d), lambda i, j: (i, j, 0)),
        pl.BlockSpec((None, HALO, d), lambda i, j: (i, jnp.maximum(j * nh - 1, 0), 0)),
        pl.BlockSpec((None, HALO, d), lambda i, j: (i, jnp.minimum((j + 1) * nh, last), 0)),
    ]


def _normed_with_halo(x_ref, xp_ref, xnx_ref, g):
    j = pl.program_id(1)
    has_prev = jnp.where(j > 0, 1.0, 0.0)
    has_next = jnp.where(j < pl.num_programs(1) - 1, 1.0, 0.0)
    return jnp.concatenate([_rms(xp_ref[...], g) * has_prev, _rms(x_ref[...], g), _rms(xnx_ref[...], g) * has_next],
                           axis=0).astype(BF16)


def _token_conv3(h, w, tm):
    ext = tm + 2 * HALO
    hp = pltpu.roll(h, 1, 0)[HALO:HALO + tm]
    hn = pltpu.roll(h, ext - 1, 0)[HALO:HALO + tm]
    return hp * w[0:1] + h[HALO:HALO + tm] * w[1:2] + hn * w[2:3]


def _in_proj_kernel(x_ref, xp_ref, xnx_ref, g_ref, wm_ref, wg_ref, cw_ref, prm_ref, na_ref, dn_ref, z_ref, gate_ref,
                    *, tm):
    xe = _normed_with_halo(x_ref, xp_ref, xnx_ref, g_ref[...])
    xm = xe[HALO:HALO + tm]

    def na_part(j):
        r = _dot(xm, wm_ref[:, j * D_NA:(j + 1) * D_NA])
        if j == 0:
            r = r * (HEAD_DIM ** -0.5)
        na_ref[:, j * D_NA:(j + 1) * D_NA] = r.astype(na_ref.dtype)

    def z_part():
        off = 3 * D_NA + 3 * D_DN
        z_ref[...] = _dot(xm, wm_ref[:, off:off + D_DN]).astype(z_ref.dtype)

    def gate_part():
        raw = _dot(xm, wg_ref[...])
        col = lax.broadcasted_iota(jnp.int32, raw.shape, 1)
        xx = raw + prm_ref[1:2]
        softplus = jnp.maximum(xx, 0.0) + jnp.log(1.0 + jnp.exp(-jnp.abs(xx)))
        gate_ref[...] = jnp.where(col < 2 * N_HEADS_DN, _sigmoid(raw), -jnp.exp(prm_ref[0:1]) * softplus)

    fillers = [functools.partial(na_part, 0), functools.partial(na_part, 1), functools.partial(na_part, 2),
               z_part, gate_part]

    ones_bd = _head_block_ones(MXU_DIM)
    n_blocks = 3 * D_DN // MXU_DIM

    def dn_proj(c):
        off = 3 * D_NA + c * MXU_DIM
        return _dot(xe, wm_ref[:, off:off + MXU_DIM])

    h_next = dn_proj(0)
    for c in range(n_blocks):
        h = h_next
        if c + 1 < n_blocks:
            h_next = dn_proj(c + 1)
        if fillers:
            fillers.pop(0)()
        cols = slice(c * MXU_DIM, (c + 1) * MXU_DIM)
        y = _token_conv3(h, cw_ref[:, cols], tm)
        y = y * _sigmoid(y)
        if c < 2 * D_DN // MXU_DIM:
            y = y * lax.rsqrt(_dot((y * y).astype(BF16), ones_bd) + L2_EPS)
        if c < D_DN // MXU_DIM:
            y = y * (HEAD_DIM ** -0.5)
        dn_ref[:, cols] = y.astype(dn_ref.dtype)
    for f in fillers:
        f()


def _in_proj(x, g, w_main, w_gate, conv_w, prm):
    b, t, _ = x.shape
    tm = _token_block(t)
    tok = lambda d: pl.BlockSpec((None, tm, d), lambda i, j: (i, j, 0))
    return pl.pallas_call(
        functools.partial(_in_proj_kernel, tm=tm),
        grid=(b, t // tm),
        in_specs=_halo_specs(tm, t, D_MODEL) + [
            _const_spec((1, D_MODEL)),
            _const_spec((D_MODEL, D_MAIN)),
            _const_spec((D_MODEL, LANES)),
            _const_spec((3, 3 * D_DN)),
            _const_spec((2, LANES)),
        ],
        out_specs=[tok(3 * D_NA), tok(3 * D_DN), tok(D_DN), tok(LANES)],
        out_shape=[
            jax.ShapeDtypeStruct((b, t, 3 * D_NA), BF16),
            jax.ShapeDtypeStruct((b, t, 3 * D_DN), BF16),
            jax.ShapeDtypeStruct((b, t, D_DN), BF16),
            jax.ShapeDtypeStruct((b, t, LANES), F32),
        ],
        compiler_params=_params(("parallel", "parallel")),
        name="in_proj",
    )(x, x, x, g, w_main, w_gate, conv_w, prm)


def _na_bias_tables(rpb_l):
    c = np.arange(GRID_W)
    cs = np.clip(c - WIN_W // 2, 0, GRID_W - WIN_W)
    valid = (c[None, :] >= cs[:, None]) & (c[None, :] < cs[:, None] + WIN_W)
    coff = np.clip(c[None, :] - c[:, None] + (WIN_W - 1), 0, 2 * WIN_W - 2)
    roff = np.arange(WIN_H)[None, :] + (WIN_H - 1) - np.arange(WIN_H)[:, None]
    select = np.asarray(coff[..., None] == np.arange(2 * WIN_W - 1), np.float32)
    b = jnp.einsum("vhib,qkb->vhqik", jnp.transpose(rpb_l[:, roff], (1, 0, 2, 3)), select,
                   precision=lax.Precision.HIGHEST)
    b = jnp.where(valid[None, None, :, None, :], b, -jnp.inf)
    return b.reshape(WIN_H, N_PAIRS, 2 * GRID_W, WIN_H * GRID_W).astype(F32)


def _na_kernel(q_ref, k_ref, v_ref, bias_ref, o_ref, *, rows):
    step = pl.program_id(1)
    nk = WIN_H * GRID_W
    lane = lax.broadcasted_iota(jnp.int32, (GRID_W, LANES), 1)
    low = lane < HEAD_DIM

    problems = []
    for rr in range(NA_ROWS):
        r = step * NA_ROWS + rr
        rs = jnp.clip(r - WIN_H // 2, 0, rows - WIN_H)
        start = pl.multiple_of(rs * GRID_W, GRID_W)
        for p in range(N_PAIRS):
            problems.append((rr, p, r - rs, start))

    def scores(rr, p, var, start):
        cols = slice(p * LANES, (p + 1) * LANES)
        q2 = q_ref[rr * GRID_W:(rr + 1) * GRID_W, cols].astype(F32)
        qs = jnp.concatenate([jnp.where(low, q2, 0.0), jnp.where(low, 0.0, q2)], axis=0).astype(BF16)
        return _dot_nt(qs, k_ref[pl.ds(start, nk), cols]) + bias_ref[var, p]

    pending = [scores(*prob) for prob in problems[:NA_AHEAD]]
    for i, (rr, p, var, start) in enumerate(problems):
        s = pending.pop(0)
        if i + NA_AHEAD < len(problems):
            pending.append(scores(*problems[i + NA_AHEAD]))
        e = jnp.exp(s - jnp.max(s, axis=-1, keepdims=True))
        l = jnp.sum(e, axis=-1, keepdims=True)
        cols = slice(p * LANES, (p + 1) * LANES)
        o = _dot(e.astype(BF16), v_ref[pl.ds(start, nk), cols]) * (1.0 / l)
        o_ref[rr * GRID_W:(rr + 1) * GRID_W, cols] = jnp.where(low, o[:GRID_W], o[GRID_W:]).astype(o_ref.dtype)


def _na(qkv, bias, b, t):
    rows = t // GRID_W
    blk = NA_ROWS * GRID_W
    return pl.pallas_call(
        functools.partial(_na_kernel, rows=rows),
        grid=(b, rows // NA_ROWS),
        in_specs=[
            pl.BlockSpec((None, blk, D_NA), lambda i, r: (i, r, 0)),
            pl.BlockSpec((None, t, D_NA), lambda i, r: (i, 0, 1)),
            pl.BlockSpec((None, t, D_NA), lambda i, r: (i, 0, 2)),
            _const_spec((WIN_H, N_PAIRS, 2 * GRID_W, WIN_H * GRID_W)),
        ],
        out_specs=pl.BlockSpec((None, blk, D_NA), lambda i, r: (i, r, 0)),
        out_shape=jax.ShapeDtypeStruct((b, t, D_NA), BF16),
        compiler_params=_params(("parallel", "arbitrary")),
        name="na",
    )(qkv, qkv, qkv, bias)


def _block_diag(x, low):
    return jnp.concatenate([jnp.where(low, x, 0.0), jnp.where(low, 0.0, x)], axis=0)


def _mm(a, b):
    return _dot(a.astype(BF16), b.astype(BF16))


def _neumann(nmats, low, eye2, between):
    ps = [_mm(nm, _block_diag(nm, low)) for nm in nmats]
    ts = [eye2 + nm for nm in nmats]
    between()
    for _ in range(4):
        outs = [_mm(p, jnp.concatenate([_block_diag(p, low), _block_diag(tt, low)], axis=1)) for p, tt in zip(ps, ts)]
        ps = [o[:, :LANES] for o in outs]
        ts = [tt + o[:, LANES:] for tt, o in zip(ts, outs)]
        between()
    res = [tt + _mm(p, _block_diag(tt, low)) for p, tt in zip(ps, ts)]
    between()
    return res


def _dn_kernel(q_ref, k_ref, v_ref, gate_ref, z_ref, normo_ref, o_ref, m_s, n_s, p_s, r_s, gl_s, out_s,
               *, t, n_items):
    n = t // CHUNK
    group = min(DN_GROUP, n)
    grp_rows = group * CHUNK
    step = pl.program_id(0)
    cur = step % 2
    prev = 1 - cur
    pair = jnp.minimum(step, n_items - 1) % N_PAIRS

    @pl.when(step == 0)
    def _():
        m_s[1] = jnp.zeros(m_s.shape[1:], m_s.dtype)
        n_s[1] = jnp.zeros(n_s.shape[1:], n_s.dtype)
        p_s[1] = jnp.zeros(p_s.shape[1:], p_s.dtype)
        r_s[1] = jnp.zeros(r_s.shape[1:], r_s.dtype)
        gl_s[1] = jnp.zeros(gl_s.shape[1:], gl_s.dtype)

    er = lax.broadcasted_iota(jnp.int32, (LANES, 4 * LANES), 0)
    ec = lax.broadcasted_iota(jnp.int32, (LANES, 4 * LANES), 1)
    src = (ec // LANES) * N_HEADS_DN + 2 * pair + (ec % LANES) // HEAD_DIM
    spread = (er == src).astype(BF16)
    spread2 = jnp.concatenate([spread, spread], axis=0)

    lane = lax.broadcasted_iota(jnp.int32, (CHUNK, LANES), 1)
    low = lane < HEAD_DIM
    ri = lax.broadcasted_iota(jnp.int32, (CHUNK, LANES), 0)
    ci = lane % HEAD_DIM
    eye2 = (ri == ci).astype(F32)
    ti = lax.broadcasted_iota(jnp.int32, (CHUNK, CHUNK), 0)
    tj = lax.broadcasted_iota(jnp.int32, (CHUNK, CHUNK), 1)
    neg_ones = jnp.full((CHUNK, CHUNK), -1.0, F32)

    dirs = []
    for tri, keep, upto, strict, last in (
            ((tj <= ti), ci <= ri, ci >= ri, ci < ri, CHUNK - 1),
            ((tj >= ti), ci >= ri, ci <= ri, ci > ri, 0)):
        trif = tri.astype(F32)
        lhs = jnp.concatenate([trif, neg_ones, trif, neg_ones], axis=1).astype(BF16)
        dirs.append((lhs, keep, upto, strict, last))

    def body(grp, states):
        states = list(states)
        done = [0]

        def recurrence_steps(count):
            for _ in range(count):
                k = grp * group + done[0]
                done[0] += 1
                for d, c in enumerate((k, n - 1 - k)):
                    rows = pl.ds(pl.multiple_of(c * CHUNK, CHUNK), CHUNK)
                    s2 = states[d]
                    lhs = jnp.concatenate([m_s[prev, d, c], p_s[prev, d, rows, :]], axis=0)
                    ms_ps = _dot(lhs, _block_diag(s2, low).astype(BF16))
                    out_s[d, rows, :] = ms_ps[CHUNK:] + r_s[prev, d, rows, :]
                    states[d] = s2 * gl_s[prev, d, c][0:1, :] + n_s[prev, d, c].astype(F32) - ms_ps[:CHUNK]

        rounds = 10
        schedule = iter([group // rounds + (1 if i < group % rounds else 0) for i in range(rounds)])
        between = lambda: recurrence_steps(next(schedule))

        r0 = pl.multiple_of(grp * grp_rows, grp_rows)
        rows_g = pl.ds(r0, grp_rows)
        kf = k_ref[rows_g, :].astype(F32)
        qf = q_ref[rows_g, :].astype(F32)
        vf = v_ref[rows_g, :].astype(F32)
        ghi, glo = _split2(gate_ref[rows_g, :])
        sp = _dot(jnp.concatenate([ghi, glo], axis=1), spread2)

        chunks = []
        for u in range(group):
            sl = slice(u * CHUNK, (u + 1) * CHUNK)
            kc, qc, vc = kf[sl], qf[sl], vf[sl]
            kq = jnp.concatenate([kc, qc], axis=0).astype(BF16)
            gq = _dot_nt(kq, _block_diag(kc, low).astype(BF16))
            chunks.append((kc, qc, vc, gq[:CHUNK], gq[CHUNK:], sp[sl]))
        between()

        dgs = []
        for kc, qc, vc, gram, qk, spc in chunks:
            for d, (lhs, keep, upto, strict, last) in enumerate(dirs):
                ghi2, glo2 = _split2(spc[:, (2 + d) * LANES:(3 + d) * LANES])
                zero = jnp.zeros_like(ghi2)
                rhs = jnp.concatenate([
                    jnp.concatenate([ghi2, jnp.where(upto, ghi2, zero), glo2, jnp.where(upto, glo2, zero)], axis=0),
                    jnp.concatenate([ghi2, zero, glo2, zero], axis=0)], axis=1)
                dgs.append(_dot(lhs, rhs))
        between()

        probs = []
        nmats = []
        it = iter(dgs)
        for kc, qc, vc, gram, qk, spc in chunks:
            for d, (lhs, keep, upto, strict, last) in enumerate(dirs):
                dg = next(it)
                beta = spc[:, d * LANES:(d + 1) * LANES]
                delta = dg[:, :LANES]
                gcol = dg[:, LANES:]
                decay = jnp.where(keep, jnp.exp(jnp.minimum(delta, 0.0)), 0.0)
                gamma = jnp.exp(gcol)
                glast = gcol[last:last + 1, :]
                nmats.append(jnp.where(strict, -(beta * gram * decay), 0.0))
                kd = kc * jnp.exp(glast - gcol)
                kd_t = jnp.transpose(jnp.concatenate([kd, jnp.zeros_like(kd)], axis=0))
                kdt = kd_t[:HEAD_DIM] + pltpu.roll(kd_t[HEAD_DIM:], HEAD_DIM, 1)
                probs.append((kc, qc, vc, beta, gamma, glast, qk * decay, kdt))
        tinvs = _neumann(nmats, low, eye2, between)

        uws = []
        for (kc, qc, vc, beta, gamma, glast, qkd, kdt), tinv in zip(probs, tinvs):
            rhs = jnp.concatenate([_block_diag(kc * (beta * gamma), low), _block_diag(vc * beta, low)], axis=1)
            uws.append(_mm(tinv, rhs))
        between()

        mnprs = []
        for (kc, qc, vc, beta, gamma, glast, qkd, kdt), wu in zip(probs, uws):
            lhs = jnp.concatenate([kdt, qkd], axis=0)
            rhs = jnp.concatenate([_block_diag(wu[:, :LANES], low), _block_diag(wu[:, LANES:], low)], axis=1)
            mnprs.append(_mm(lhs, rhs))
        between()

        idx = 0
        for u in range(group):
            c = grp * group + u
            rows = pl.ds(pl.multiple_of(r0 + u * CHUNK, CHUNK), CHUNK)
            for d in range(2):
                kc, qc, vc, beta, gamma, glast, qkd, kdt = probs[idx]
                mn, pr = mnprs[idx][:CHUNK], mnprs[idx][CHUNK:]
                m_s[cur, d, c] = mn[:, :LANES].astype(BF16)
                n_s[cur, d, c] = mn[:, LANES:].astype(BF16)
                p_s[cur, d, rows, :] = (qc * gamma - pr[:, :LANES]).astype(BF16)
                r_s[cur, d, rows, :] = pr[:, LANES:]
                gl_s[cur, d, c] = jnp.broadcast_to(jnp.exp(glast), (8, LANES))
                idx += 1
        assert done[0] == group
        return tuple(states)

    zero_state = jnp.zeros((CHUNK, LANES), F32)
    lax.fori_loop(0, n // group, body, (zero_state, zero_state), unroll=DN_UNROLL)

    o = out_s[0] + out_s[1]
    sq = (o * o).astype(BF16)
    ss = _dot(jnp.concatenate([sq[:t // 2], sq[t // 2:]], axis=1), _head_block_ones(MXU_DIM))
    ms = jnp.concatenate([ss[:, :LANES], ss[:, LANES:]], axis=0) * (1.0 / HEAD_DIM)
    o = o * lax.rsqrt(ms + RMS_EPS) * normo_ref[...]
    z = z_ref[...].astype(F32)
    o_ref[...] = (o * (z * _sigmoid(z))).astype(o_ref.dtype)


def _dn(qkv, z, gates, normo2, b, t):
    n = t // CHUNK
    n_items = b * N_PAIRS

    def cur_spec(base):
        def index(s):
            i = jnp.minimum(s, n_items - 1)
            return (i // N_PAIRS, 0, base + i % N_PAIRS)
        return pl.BlockSpec((None, t, LANES), index)

    def lag_index(s):
        i = jnp.maximum(s - 1, 0)
        return (i // N_PAIRS, 0, i % N_PAIRS)

    return pl.pallas_call(
        functools.partial(_dn_kernel, t=t, n_items=n_items),
        grid=(n_items + 1,),
        in_specs=[
            cur_spec(0), cur_spec(N_PAIRS), cur_spec(2 * N_PAIRS),
            pl.BlockSpec((None, t, LANES), lambda s: (jnp.minimum(s, n_items - 1) // N_PAIRS, 0, 0)),
            pl.BlockSpec((None, t, LANES), lag_index),
            _const_spec((1, LANES)),
        ],
        out_specs=pl.BlockSpec((None, t, LANES), lag_index),
        out_shape=jax.ShapeDtypeStruct((b, t, D_DN), BF16),
        scratch_shapes=[
            pltpu.VMEM((2, 2, n, CHUNK, LANES), BF16),
            pltpu.VMEM((2, 2, n, CHUNK, LANES), BF16),
            pltpu.VMEM((2, 2, t, LANES), BF16),
            pltpu.VMEM((2, 2, t, LANES), F32),
            pltpu.VMEM((2, 2, n, 8, LANES), F32),
            pltpu.VMEM((2, t, LANES), F32),
        ],
        compiler_params=_params(("arbitrary",)),
        name="dn",
    )(qkv, qkv, qkv, gates, z, normo2)


def _mem_kv_kernel(m_ref, g_ref, w_ref, k_ref, v_ref):
    mn = _rms(m_ref[...], g_ref[...]).astype(BF16)
    k_ref[...] = _dot(mn, w_ref[:, :D_MODEL]).astype(k_ref.dtype)
    v_ref[...] = _dot(mn, w_ref[:, D_MODEL:]).astype(v_ref.dtype)


def _mem_kv(mem, g, w_kv):
    b = mem.shape[0]
    blk = pl.BlockSpec((None, N_MEM, D_MODEL), lambda i: (i, 0, 0))
    return pl.pallas_call(
        _mem_kv_kernel,
        grid=(b,),
        in_specs=[blk, _const_spec((1, D_MODEL)), _const_spec((D_MODEL, 2 * D_MODEL))],
        out_specs=[blk, blk],
        out_shape=[jax.ShapeDtypeStruct((b, N_MEM, D_MODEL), BF16)] * 2,
        compiler_params=_params(("parallel",)),
        name="mem_kv",
    )(mem, g, w_kv)


def _mix_xattn_kernel(x_ref, yna_ref, ydn_ref, wout_ref, g_ref, wq_ref, k_ref, v_ref, wo_ref, o_ref):
    x = x_ref[...] + _dot(yna_ref[...], wout_ref[:D_NA, :]) + _dot(ydn_ref[...], wout_ref[D_NA:, :])
    xn = _rms(x, g_ref[...]).astype(BF16)
    q = (_dot(xn, wq_ref[...]) * (HEAD_DIM_X ** -0.5)).astype(BF16)
    head_cols = [slice(h * HEAD_DIM_X, (h + 1) * HEAD_DIM_X) for h in range(N_HEADS_X)]
    scores = [_dot_nt(q[:, cols], k_ref[:, cols]) for cols in head_cols]
    probs = []
    for s in scores:
        e = jnp.exp(s - jnp.max(s, axis=-1, keepdims=True))
        probs.append((e.astype(BF16), jnp.sum(e, axis=-1, keepdims=True)))
    heads = [(_dot(e, v_ref[:, cols]) * (1.0 / l)).astype(BF16) for (e, l), cols in zip(probs, head_cols)]
    o_ref[...] = x + _dot(jnp.concatenate(heads, axis=-1), wo_ref[...])


def _mix_xattn(x, y_na, y_dn, w_out, g, w_q, kmem, vmem, w_o):
    b, t, _ = x.shape
    tm = _token_block(t)
    tok = lambda d: pl.BlockSpec((None, tm, d), lambda i, j: (i, j, 0))
    memspec = pl.BlockSpec((None, N_MEM, D_MODEL), lambda i, j: (i, 0, 0))
    return pl.pallas_call(
        _mix_xattn_kernel,
        grid=(b, t // tm),
        in_specs=[
            tok(D_MODEL), tok(D_NA), tok(D_DN),
            _const_spec((D_NA + D_DN, D_MODEL)),
            _const_spec((1, D_MODEL)),
            _const_spec((D_MODEL, D_MODEL)),
            memspec, memspec,
            _const_spec((D_MODEL, D_MODEL)),
        ],
        out_specs=tok(D_MODEL),
        out_shape=jax.ShapeDtypeStruct((b, t, D_MODEL), F32),
        compiler_params=_params(("parallel", "parallel")),
        name="mix_xattn",
    )(x, y_na, y_dn, w_out, g, w_q, kmem, vmem, w_o)


def _ffn_kernel(x_ref, xp_ref, xnx_ref, g_ref, wu_ref, cw_ref, cb_ref, wd_ref, gf_ref, o_ref, *, tm, final):
    xe = _normed_with_halo(x_ref, xp_ref, xnx_ref, g_ref[...])
    val_cols = lambda c: slice(c * FF_CHUNK, (c + 1) * FF_CHUNK)
    gate_cols = lambda c: slice(D_FF + c * FF_CHUNK, D_FF + (c + 1) * FF_CHUNK)
    up = lambda c: (_dot(xe, wu_ref[:, val_cols(c)]), _dot(xe, wu_ref[:, gate_cols(c)]))
    ups = [up(c) for c in range(FFN_AHEAD)]
    acts = []
    for c in range(N_FF_CHUNKS):
        hv, hg = ups.pop(0)
        if c + FFN_AHEAD < N_FF_CHUNKS:
            ups.append(up(c + FFN_AHEAD))
        val = _token_conv3(hv, cw_ref[:, val_cols(c)], tm) + cb_ref[:, val_cols(c)]
        gate = _token_conv3(hg, cw_ref[:, gate_cols(c)], tm) + cb_ref[:, gate_cols(c)]
        acts.append((gate * _sigmoid(gate) * val).astype(BF16))
    acc = x_ref[...] + _dot(jnp.concatenate(acts, axis=1), wd_ref[...])
    if final:
        acc = _rms(acc, gf_ref[...])
    o_ref[...] = acc


def _ffn(x, g, w_up, conv_w, conv_b, w_down, g_final, final):
    b, t, _ = x.shape
    tm = _token_block(t)
    return pl.pallas_call(
        functools.partial(_ffn_kernel, tm=tm, final=final),
        grid=(b, t // tm),
        in_specs=_halo_specs(tm, t, D_MODEL) + [
            _const_spec((1, D_MODEL)),
            _const_spec((D_MODEL, 2 * D_FF)),
            _const_spec((3, 2 * D_FF)),
            _const_spec((1, 2 * D_FF)),
            _const_spec((D_FF, D_MODEL)),
            _const_spec((1, D_MODEL)),
        ],
        out_specs=pl.BlockSpec((None, tm, D_MODEL), lambda i, j: (i, j, 0)),
        out_shape=jax.ShapeDtypeStruct((b, t, D_MODEL), F32),
        compiler_params=_params(("parallel", "parallel")),
        name="ffn",
    )(x, x, x, g, w_up, conv_w, conv_b, w_down, g_final)


def _prep_layer(l, norm_mix, w_in, rpb, conv_qkv, a_log, dt_bias, norm_o, w_out, norm_x, norm_mem, w_xq, w_xkv, w_xo,
                norm_ffn, w_up, conv_ffn, conv_ffn_b, w_down):
    row = lambda v: v.reshape(1, -1).astype(F32)
    w_gate = jnp.pad(w_in[l][:, D_MAIN:], ((0, 0), (0, LANES - N_GATE))).astype(BF16)
    pad = jnp.zeros((2 * N_HEADS_DN,), F32)
    tail = jnp.zeros((LANES - N_GATE,), F32)
    prm = jnp.stack([jnp.concatenate([pad, a_log[l].reshape(-1), tail]),
                     jnp.concatenate([pad, dt_bias[l].reshape(-1), tail])])
    return dict(
        norm_mix=row(norm_mix[l]), w_main=w_in[l][:, :D_MAIN].astype(BF16), w_gate=w_gate,
        na_bias=_na_bias_tables(rpb[l]), conv_qkv=conv_qkv[l], prm=prm,
        norm_o=row(jnp.concatenate([norm_o[l], norm_o[l]])), w_out=w_out[l].astype(BF16),
        norm_x=row(norm_x[l]), norm_mem=row(norm_mem[l]), w_xq=w_xq[l].astype(BF16), w_xkv=w_xkv[l].astype(BF16),
        w_xo=w_xo[l].astype(BF16), norm_ffn=row(norm_ffn[l]), w_up=w_up[l].astype(BF16), conv_ffn=conv_ffn[l],
        conv_ffn_b=row(conv_ffn_b[l]), w_down=w_down[l].astype(BF16))


def _layer(x, mem, p, g_final, final):
    b, t, _ = x.shape
    qkv_na, qkv_dn, z, gates = _in_proj(x, p["norm_mix"], p["w_main"], p["w_gate"], p["conv_qkv"], p["prm"])
    y_na = _na(qkv_na, p["na_bias"], b, t)
    y_dn = _dn(qkv_dn, z, gates, p["norm_o"], b, t)
    kmem, vmem = _mem_kv(mem, p["norm_mem"], p["w_xkv"])
    x = _mix_xattn(x, y_na, y_dn, p["w_out"], p["norm_x"], p["w_xq"], kmem, vmem, p["w_xo"])
    return _ffn(x, p["norm_ffn"], p["w_up"], p["conv_ffn"], p["conv_ffn_b"], p["w_down"], g_final, final)


def kernel(x_prompt, x_sample, mem_prompt, mem_sample, norm_mix, w_in, rpb, conv_qkv, a_log, dt_bias, norm_o, w_out,
           norm_x, norm_mem, w_xq, w_xkv, w_xo, norm_ffn, w_up, conv_ffn, conv_ffn_b, w_down, norm_final):
    layers = [_prep_layer(l, norm_mix, w_in, rpb, conv_qkv, a_log, dt_bias, norm_o, w_out, norm_x, norm_mem, w_xq,
                          w_xkv, w_xo, norm_ffn, w_up, conv_ffn, conv_ffn_b, w_down) for l in range(DEPTH)]
    g_final = norm_final.reshape(1, -1).astype(F32)
    outs = []
    for x, mem in ((x_prompt, mem_prompt), (x_sample, mem_sample)):
        for l in range(DEPTH):
            x = _layer(x, mem, layers[l], g_final, l == DEPTH - 1)
        outs.append(x)
    return tuple(outs)
```
